```python
import functools
import jax, jax.numpy as jnp
from jax import lax
import numpy as np

D_MODEL = 1024
BATCH = 4
SEQ = 8192
DEPTH = 1
DEC_BATCH = 128
DEC_SEQ = 8
PAST_LEN = 16384
PAGE_SIZE = 128

MLA_HEADS = 8
QK_NOPE = 64
QK_ROPE = 32
V_HEAD = 64
KV_LORA = 128
Q_LORA = 256
MLA_WIDTH = MLA_HEADS * V_HEAD
ROPE_THETA = 10000.0
Q_BLOCK = 128
GLA_HEADS = 4
GLA_DK = 64
GLA_DV = 128
GLA_WIDTH = GLA_HEADS * GLA_DV
GLA_GATE_RANK = 16
GLA_TAU = 16.0
GLA_CHUNK = 64
P_DIM = 256
NORM_EPS = 1e-6
MIX_WIDTH = MLA_WIDTH + GLA_WIDTH
IN_SIZES = (Q_LORA, KV_LORA + QK_ROPE, MLA_WIDTH, GLA_HEADS * GLA_DK, GLA_HEADS * GLA_DK,
            GLA_WIDTH, GLA_GATE_RANK, GLA_WIDTH)
IN_DIM = Q_LORA + KV_LORA + QK_ROPE + MLA_WIDTH + 2 * GLA_HEADS * GLA_DK + GLA_WIDTH + GLA_GATE_RANK + GLA_WIDTH

kernel_name = 'hymba_mla_gla_decode_step'


def rmsnorm(x, g):
    xf = x.astype(jnp.float32)
    y = xf * lax.rsqrt(jnp.mean(xf * xf, axis=-1, keepdims=True) + NORM_EPS)
    return (y * g.astype(jnp.float32)).astype(x.dtype)


def rope_tables(pos):
    inv = 1.0 / (ROPE_THETA ** (jnp.arange(0, QK_ROPE, 2, dtype=jnp.float32) / QK_ROPE))
    ang = pos[:, None] * inv[None, :]
    return jnp.cos(ang), jnp.sin(ang)


def apply_rope(x, cos, sin):
    x1, x2 = jnp.split(x.astype(jnp.float32), 2, axis=-1)
    return jnp.concatenate([x1 * cos - x2 * sin, x1 * sin + x2 * cos], axis=-1).astype(x.dtype)


def mla_prompt_attention(q_nope, q_rope, c_kv, k_rope, w_uk, w_uv):
    B, S, H, _ = q_nope.shape
    scale = (QK_NOPE + QK_ROPE) ** -0.5
    k_nope = jnp.einsum('bsc,chd->bshd', c_kv, w_uk)
    v = jnp.einsum('bsc,chd->bshd', c_kv, w_uv)
    nb = S // Q_BLOCK
    qn = q_nope.reshape(B, nb, Q_BLOCK, H, QK_NOPE).transpose(1, 0, 2, 3, 4)
    qr = q_rope.reshape(B, nb, Q_BLOCK, H, QK_ROPE).transpose(1, 0, 2, 3, 4)
    kpos = jnp.arange(S)

    def block(args):
        qn_b, qr_b, start = args
        s = (jnp.einsum('bqhd,bkhd->bhqk', qn_b, k_nope)
             + jnp.einsum('bqhr,bkr->bhqk', qr_b, k_rope)).astype(jnp.float32) * scale
        qpos = start + jnp.arange(Q_BLOCK)
        s = jnp.where(kpos[None, :] <= qpos[:, None], s, -jnp.inf)
        p = jax.nn.softmax(s, axis=-1).astype(v.dtype)
        return jnp.einsum('bhqk,bkhd->bqhd', p, v)

    out = lax.map(block, (qn, qr, jnp.arange(nb) * Q_BLOCK))
    return out.transpose(1, 0, 2, 3, 4).reshape(B, S, H, V_HEAD)


def mla_decode_attention(q_nope, q_rope, c_new, kr_new, w_uk, w_uv, ckv_pool, kr_pool, page_table):
    T = q_nope.shape[1]
    scale = (QK_NOPE + QK_ROPE) ** -0.5
    past_len = page_table.shape[1] * ckv_pool.shape[1]
    q_lat = jnp.einsum('bthd,chd->bthc', q_nope, w_uk)
    kpos = jnp.arange(past_len + T)
    qpos = past_len + jnp.arange(T)
    mask = kpos[None, :] <= qpos[:, None]

    def one(args):
        ql, qr, pages, cn, krn = args
        c = jnp.concatenate([ckv_pool[pages].reshape(-1, KV_LORA), cn.astype(ckv_pool.dtype)], axis=0)
        kr = jnp.concatenate([kr_pool[pages].reshape(-1, QK_ROPE), krn.astype(kr_pool.dtype)], axis=0)
        s = (jnp.einsum('thc,lc->htl', ql, c)
             + jnp.einsum('thr,lr->htl', qr, kr)).astype(jnp.float32) * scale
        s = jnp.where(mask[None], s, -jnp.inf)
        p = jax.nn.softmax(s, axis=-1).astype(c.dtype)
        return jnp.einsum('htl,lc->thc', p, c)

    o_lat = lax.map(one, (q_lat, q_rope, page_table, c_new, kr_new))
    return jnp.einsum('bthc,chd->bthd', o_lat, w_uv.astype(o_lat.dtype))


def gla_chunked(q, k, v, log_a, s0, chunk):
    B, T, H, DK = q.shape
    DV = v.shape[-1]
    n = T // chunk
    f32 = jnp.float32
    q = q.astype(f32) * (DK ** -0.5)

    def to_chunks(a):
        return a.astype(f32).reshape(B, n, chunk, H, a.shape[-1]).transpose(1, 0, 3, 2, 4)

    tri = jnp.tril(jnp.ones((chunk, chunk), dtype=bool))

    def step(S, inp):
        qc, kc, vc, gc = inp
        b = jnp.cumsum(gc, axis=-2)
        o_inter = jnp.einsum('bhtk,bhkv->bhtv', qc * jnp.exp(b), S)
        diff = b[:, :, :, None, :] - b[:, :, None, :, :]
        decay = jnp.exp(jnp.where(tri[None, None, :, :, None], diff, -jnp.inf))
        A = jnp.einsum('bhtk,bhsk,bhtsk->bhts', qc, kc, decay)
        o = o_inter + jnp.einsum('bhts,bhsv->bhtv', A, vc)
        b_last = b[:, :, -1:, :]
        S = (jnp.exp(b_last[:, :, 0, :])[..., None] * S
             + jnp.einsum('bhsk,bhsv->bhkv', kc * jnp.exp(b_last - b), vc))
        return S, o

    S, o = lax.scan(step, s0.astype(f32), (to_chunks(q), to_chunks(k), to_chunks(v), to_chunks(log_a)))
    return o.transpose(1, 0, 3, 2, 4).reshape(B, T, H, DV), S


def hybrid_layer(h, p_l, pos, gla_s0, gla_chunk, mla_attend, g_mix_norm, w_in, g_qnorm, w_qup,
                 g_kvnorm, w_uk, w_uv, w_gla_a2, b_gla_a, g_gla_onorm, w_out, w_ple_gate, w_ple_proj):
    B, T, _ = h.shape
    xn = rmsnorm(h, g_mix_norm)
    z = xn @ w_in
    bounds = np.cumsum(IN_SIZES)[:-1].tolist()
    q_c, kv_c, gate_mla, q_g, k_g, v_g, a_g, gate_gla = jnp.split(z, bounds, axis=-1)
    cos, sin = rope_tables(pos)
    q = (rmsnorm(q_c, g_qnorm) @ w_qup).reshape(B, T, MLA_HEADS, QK_NOPE + QK_ROPE)
    q_nope = q[..., :QK_NOPE]
    q_rope = apply_rope(q[..., QK_NOPE:], cos[:, None, :], sin[:, None, :])
    c_kv = rmsnorm(kv_c[..., :KV_LORA], g_kvnorm)
    k_rope = apply_rope(kv_c[..., KV_LORA:], cos, sin)
    o_mla = mla_attend(q_nope, q_rope, c_kv, k_rope, w_uk, w_uv).reshape(B, T, MLA_WIDTH).astype(h.dtype)
    o_mla = o_mla * jax.nn.silu(gate_mla)
    gq = q_g.reshape(B, T, GLA_HEADS, GLA_DK)
    gk = k_g.reshape(B, T, GLA_HEADS, GLA_DK)
    gv = v_g.reshape(B, T, GLA_HEADS, GLA_DV)
    log_a = (jax.nn.log_sigmoid((a_g @ w_gla_a2 + b_gla_a).astype(jnp.float32)) / GLA_TAU
             ).reshape(B, T, GLA_HEADS, GLA_DK)
    o_gla, s_new = gla_chunked(gq, gk, gv, log_a, gla_s0, gla_chunk)
    o_gla = rmsnorm(o_gla.astype(h.dtype), g_gla_onorm).reshape(B, T, GLA_WIDTH) * jax.nn.silu(gate_gla)
    h = h + jnp.concatenate([o_mla, o_gla], axis=-1) @ w_out
    h = h + jax.nn.sigmoid(h @ w_ple_gate) * (p_l @ w_ple_proj)
    return h, c_kv, k_rope, s_new.astype(h.dtype)


def setup_inputs(seed: int = 0) -> dict:
    key = jax.random.key(seed)
    ks = jax.random.split(key, 32)
    n_pages = PAST_LEN // PAGE_SIZE
    n_pool = (DEC_BATCH * n_pages * 5) // 4
    f32 = jnp.float32

    def nrm(k, shape, scale):
        return jax.random.normal(k, shape, f32) * scale

    def gain(k, shape):
        return 1.0 + 0.05 * jax.random.normal(k, shape, f32)

    page_table = jax.random.permutation(ks[7], n_pool)[:DEC_BATCH * n_pages].reshape(DEC_BATCH, n_pages).astype(jnp.int32)
    return {
        'x_prompt': nrm(ks[0], (BATCH, SEQ, D_MODEL), 1.0),
        'x_sample': nrm(ks[1], (DEC_BATCH, DEC_SEQ, D_MODEL), 1.0),
        'p_prompt': nrm(ks[2], (DEPTH, BATCH, SEQ, P_DIM), 1.0),
        'p_sample': nrm(ks[3], (DEPTH, DEC_BATCH, DEC_SEQ, P_DIM), 1.0),
        'cache_ckv': nrm(ks[4], (DEPTH, n_pool, PAGE_SIZE, KV_LORA), 1.0),
        'cache_krope': nrm(ks[5], (DEPTH, n_pool, PAGE_SIZE, QK_ROPE), 1.0),
        'state_gla': nrm(ks[6], (DEPTH, DEC_BATCH, GLA_HEADS, GLA_DK, GLA_DV), 1.0),
        'page_table': page_table,
        'g_mix_norm': gain(ks[8], (DEPTH, D_MODEL)),
        'w_in': nrm(ks[9], (DEPTH, D_MODEL, IN_DIM), D_MODEL ** -0.5),
        'g_qnorm': gain(ks[10], (DEPTH, Q_LORA)),
        'w_qup': nrm(ks[11], (DEPTH, Q_LORA, MLA_HEADS * (QK_NOPE + QK_ROPE)), Q_LORA ** -0.5),
        'g_kvnorm': gain(ks[12], (DEPTH, KV_LORA)),
        'w_uk': nrm(ks[13], (DEPTH, KV_LORA, MLA_HEADS, QK_NOPE), KV_LORA ** -0.5),
        'w_uv': nrm(ks[14], (DEPTH, KV_LORA, MLA_HEADS, V_HEAD), KV_LORA ** -0.5),
        'w_gla_a2': nrm(ks[15], (DEPTH, GLA_GATE_RANK, GLA_HEADS * GLA_DK), GLA_GATE_RANK ** -0.5),
        'b_gla_a': nrm(ks[16], (DEPTH, GLA_HEADS * GLA_DK), 0.5),
        'g_gla_onorm': gain(ks[17], (DEPTH, GLA_DV)),
        'w_out': nrm(ks[18], (DEPTH, MIX_WIDTH, D_MODEL), MIX_WIDTH ** -0.5),
        'w_ple_gate': nrm(ks[19], (DEPTH, D_MODEL, D_MODEL), D_MODEL ** -0.5),
        'w_ple_proj': nrm(ks[20], (DEPTH, P_DIM, D_MODEL), P_DIM ** -0.5),
        'g_final': gain(ks[21], (D_MODEL,)),
    }


def reference(x_prompt, x_sample, p_prompt, p_sample, cache_ckv, cache_krope, state_gla, page_table,
              g_mix_norm, w_in, g_qnorm, w_qup, g_kvnorm, w_uk, w_uv, w_gla_a2, b_gla_a, g_gla_onorm,
              w_out, w_ple_gate, w_ple_proj, g_final):
    B, S, _ = x_prompt.shape
    T = x_sample.shape[1]
    past_len = page_table.shape[1] * cache_ckv.shape[2]
    pos_p = jnp.arange(S, dtype=jnp.float32)
    pos_s = past_len + jnp.arange(T, dtype=jnp.float32)
    s0_prompt = jnp.zeros((B, GLA_HEADS, GLA_DK, GLA_DV), jnp.float32)
    h_p, h_s = x_prompt, x_sample
    ckv_p, kr_p, st_p, ckv_s, kr_s, st_s = [], [], [], [], [], []
    for i in range(DEPTH):
        w = (g_mix_norm[i], w_in[i], g_qnorm[i], w_qup[i], g_kvnorm[i], w_uk[i], w_uv[i],
             w_gla_a2[i], b_gla_a[i], g_gla_onorm[i], w_out[i], w_ple_gate[i], w_ple_proj[i])
        h_p, c1, r1, s1 = hybrid_layer(h_p, p_prompt[i], pos_p, s0_prompt, min(GLA_CHUNK, S),
                                       mla_prompt_attention, *w)
        decode_attend = functools.partial(mla_decode_attention, ckv_pool=cache_ckv[i],
                                          kr_pool=cache_krope[i], page_table=page_table)
        h_s, c2, r2, s2 = hybrid_layer(h_s, p_sample[i], pos_s, state_gla[i], T, decode_attend, *w)
        ckv_p.append(c1); kr_p.append(r1); st_p.append(s1)
        ckv_s.append(c2); kr_s.append(r2); st_s.append(s2)
    y_prompt = rmsnorm(h_p, g_final)
    y_sample = rmsnorm(h_s, g_final)
    return (y_prompt, y_sample, jnp.stack(ckv_p), jnp.stack(kr_p), jnp.stack(st_p),
            jnp.stack(ckv_s), jnp.stack(kr_s), jnp.stack(st_s))
```

```python
import functools

import numpy as np
import jax
import jax.numpy as jnp
from jax import lax
from jax.experimental import pallas as pl
from jax.experimental.pallas import tpu as pltpu

F32 = jnp.float32
BF16 = jnp.bfloat16

D_MODEL = 1024
MLA_HEADS = 8
QK_NOPE = 64
QK_ROPE = 32
V_HEAD = 64
KV_LORA = 128
Q_LORA = 256
MLA_WIDTH = MLA_HEADS * V_HEAD
GLA_HEADS = 4
GLA_DK = 64
GLA_DV = 128
GLA_WIDTH = GLA_HEADS * GLA_DV
GLA_QK = GLA_HEADS * GLA_DK
GLA_GATE_RANK = 16
GLA_TAU = 16.0
P_DIM = 256
NORM_EPS = 1e-6
ROPE_THETA = 10000.0
QK_LAT = KV_LORA + QK_ROPE
LAT_WIDTH = MLA_HEADS * KV_LORA
Z_WIDTH = 2560
GLA_CHUNK = 64

LANES = 128
VMEM_LIMIT = 56 * 1024 * 1024


def _rms(x):
    return x * lax.rsqrt(jnp.mean(x * x, axis=-1, keepdims=True) + NORM_EPS)


def _silu(x):
    return x * (1.0 / (1.0 + jnp.exp(-x)))


def _swap16(x):
    n = x.shape[-1]
    lane = lax.broadcasted_iota(jnp.int32, x.shape, x.ndim - 1)
    fwd = pltpu.roll(x, n - 16, axis=x.ndim - 1)
    bwd = pltpu.roll(x, 16, axis=x.ndim - 1)
    return jnp.where((lane & 31) < 16, fwd, bwd)


def _proj_kernel(x_ref, cos_ref, sin_ref, gmix_ref, wall_ref, gqn_ref, wqup_ref, wukbd_ref,
                 gkvn_ref, wa2_ref, ba_ref,
                 q_out, kv_out, ckv_out, kr_out, gm_out, gqk_out, gv_out, la_out, gg_out,
                 *, scale):
    x = x_ref[...]
    xn = (_rms(x) * gmix_ref[...]).astype(BF16)
    z = jnp.dot(xn, wall_ref[...], preferred_element_type=F32)
    cos = cos_ref[...]
    sin = sin_ref[...]

    qn = (_rms(z[:, 0:Q_LORA]) * gqn_ref[...]).astype(BF16)
    q = jnp.dot(qn, wqup_ref[...], preferred_element_type=F32)
    nope_w = MLA_HEADS * QK_NOPE
    q_lat = jnp.dot(q[:, :nope_w].astype(BF16), wukbd_ref[...],
                    preferred_element_type=F32) * scale
    qr = q[:, nope_w:]
    cos2 = jnp.concatenate([cos, cos], axis=1)
    sin2 = jnp.concatenate([sin, sin], axis=1)
    qr = (qr * cos2 + _swap16(qr) * sin2) * scale
    nb, _, lb, _ = q_out.shape
    for h in range(MLA_HEADS):
        q_h = jnp.concatenate([q_lat[:, h * KV_LORA:(h + 1) * KV_LORA],
                               qr[:, h * QK_ROPE:(h + 1) * QK_ROPE]], axis=1)
        q_out[:, h, :, :] = q_h.reshape(nb, lb, QK_LAT).astype(q_out.dtype)

    ckv = _rms(z[:, 256:384]) * gkvn_ref[...]
    ckv_out[...] = ckv
    blk = z[:, 384:512]
    kr = (blk * cos + _swap16(blk) * sin)[:, :QK_ROPE]
    kr_out[...] = kr
    kv_out[:, 0:KV_LORA] = ckv.astype(kv_out.dtype)
    kv_out[:, KV_LORA:QK_LAT] = kr.astype(kv_out.dtype)

    a_pre = jnp.dot(blk.astype(BF16), wa2_ref[...], preferred_element_type=F32) + ba_ref[...]
    la_out[...] = (jnp.minimum(a_pre, 0.0) - jnp.log1p(jnp.exp(-jnp.abs(a_pre)))) * (1.0 / GLA_TAU)

    gm_out[...] = _silu(z[:, 512:1024]).astype(gm_out.dtype)
    gqk_out[...] = z[:, 1024:1536]
    gv_out[...] = z[:, 1536:2048].astype(gv_out.dtype)
    gg_out[...] = _silu(z[:, 2048:2560]).astype(gg_out.dtype)


def _proj_call(x2d, cos_t, sin_t, w, *, n_seq, seq_len, ts, q_dtype, table_period):
    n_tok = x2d.shape[0]
    n_tiles = n_tok // ts
    if ts >= seq_len:
        nb, lb = ts // seq_len, seq_len
        q_map = lambda i: (i, 0, 0, 0)
    else:
        nb, lb = 1, ts
        per = seq_len // ts
        q_map = lambda i: (i // per, 0, i % per, 0)
    tok = lambda width: pl.BlockSpec((ts, width), lambda i: (i, 0))
    full = lambda a: pl.BlockSpec(a.shape, lambda i: (0,) * a.ndim)
    tab = pl.BlockSpec((ts, LANES), lambda i: (i % table_period, 0))
    out_shape = (
        jax.ShapeDtypeStruct((n_seq, MLA_HEADS, seq_len, QK_LAT), q_dtype),
        jax.ShapeDtypeStruct((n_tok, QK_LAT), q_dtype),
        jax.ShapeDtypeStruct((n_tok, KV_LORA), F32),
        jax.ShapeDtypeStruct((n_tok, QK_ROPE), F32),
        jax.ShapeDtypeStruct((n_tok, MLA_WIDTH), BF16),
        jax.ShapeDtypeStruct((n_tok, 2 * GLA_QK), F32),
        jax.ShapeDtypeStruct((n_tok, GLA_WIDTH), BF16),
        jax.ShapeDtypeStruct((n_tok, GLA_QK), F32),
        jax.ShapeDtypeStruct((n_tok, GLA_WIDTH), BF16),
    )
    out_specs = (
        pl.BlockSpec((nb, MLA_HEADS, lb, QK_LAT), q_map),
        tok(QK_LAT), tok(KV_LORA), tok(QK_ROPE), tok(MLA_WIDTH), tok(2 * GLA_QK),
        tok(GLA_WIDTH), tok(GLA_QK), tok(GLA_WIDTH),
    )
    weights = (w["g_mix"], w["w_all"], w["g_qn"], w["w_qup"], w["w_uk_bd"], w["g_kvn"],
               w["w_a2"], w["b_a"])
    return pl.pallas_call(
        functools.partial(_proj_kernel, scale=(QK_NOPE + QK_ROPE) ** -0.5),
        grid=(n_tiles,),
        in_specs=[tok(D_MODEL), tab, tab] + [full(a) for a in weights],
        out_specs=out_specs,
        out_shape=out_shape,
        compiler_params=pltpu.CompilerParams(dimension_semantics=("arbitrary",),
                                             vmem_limit_bytes=VMEM_LIMIT),
        name="proj",
    )(x2d, cos_t, sin_t, *weights)


def _softmax_update(s, v, m_sc, l_sc, acc_sc):
    m_prev = m_sc[...]
    m_new = jnp.maximum(m_prev, jnp.max(s, axis=1, keepdims=True))
    alpha = jnp.exp(m_prev - m_new)
    p = jnp.exp(s - m_new)
    l_sc[...] = alpha * l_sc[...] + jnp.sum(p, axis=1, keepdims=True)
    acc_sc[...] = alpha * acc_sc[...] + jnp.dot(p.astype(BF16), v, preferred_element_type=F32)
    m_sc[...] = m_new


def _softmax_init(m_sc, l_sc, acc_sc):
    m_sc[...] = jnp.full(m_sc.shape, -jnp.inf, F32)
    l_sc[...] = jnp.zeros(l_sc.shape, F32)
    acc_sc[...] = jnp.zeros(acc_sc.shape, F32)


def _mla_prompt_kernel(qi_ref, kj_ref, q_ref, kv_ref, o_ref, m_sc, l_sc, acc_sc, *, tq, tk):
    step = pl.program_id(1)
    qi = qi_ref[step]
    kj = kj_ref[step]
    last_kj = (qi * tq + (tq - 1)) // tk

    @pl.when(kj == 0)
    def _():
        _softmax_init(m_sc, l_sc, acc_sc)

    q = q_ref[0].reshape(MLA_HEADS * tq, QK_LAT)
    kv = kv_ref[0]
    v = kv[:, :KV_LORA]
    s = lax.dot_general(q, kv, (((1,), (1,)), ((), ())), preferred_element_type=F32)

    @pl.when(kj < last_kj)
    def _():
        _softmax_update(s, v, m_sc, l_sc, acc_sc)

    @pl.when(kj == last_kj)
    def _():
        qpos = qi * tq + lax.broadcasted_iota(jnp.int32, (tq, tk), 0)
        kpos = kj * tk + lax.broadcasted_iota(jnp.int32, (tq, tk), 1)
        keep = (kpos <= qpos)[None]
        sm = jnp.where(keep, s.reshape(MLA_HEADS, tq, tk), -jnp.inf).reshape(MLA_HEADS * tq, tk)
        _softmax_update(sm, v, m_sc, l_sc, acc_sc)
        inv_l = 1.0 / l_sc[...]
        for h in range(MLA_HEADS):
            rows = slice(h * tq, (h + 1) * tq)
            o_ref[0, :, h * KV_LORA:(h + 1) * KV_LORA] = (acc_sc[rows, :] * inv_l[rows, :]).astype(o_ref.dtype)


def _mla_prompt_call(q, kv, *, tq, tk):
    n_b, _, seq, _ = q.shape
    qi_l, kj_l = [], []
    for i in range(seq // tq):
        for j in range((i * tq + tq - 1) // tk + 1):
            qi_l.append(i)
            kj_l.append(j)
    qi_t = jnp.asarray(np.array(qi_l, np.int32))
    kj_t = jnp.asarray(np.array(kj_l, np.int32))
    grid_spec = pltpu.PrefetchScalarGridSpec(
        num_scalar_prefetch=2,
        grid=(n_b, len(qi_l)),
        in_specs=[
            pl.BlockSpec((1, MLA_HEADS, tq, QK_LAT), lambda b, s, qi, kj: (b, 0, qi[s], 0)),
            pl.BlockSpec((1, tk, QK_LAT), lambda b, s, qi, kj: (b, kj[s], 0)),
        ],
        out_specs=pl.BlockSpec((1, tq, LAT_WIDTH), lambda b, s, qi, kj: (b, qi[s], 0)),
        scratch_shapes=[pltpu.VMEM((MLA_HEADS * tq, 1), F32), pltpu.VMEM((MLA_HEADS * tq, 1), F32),
                        pltpu.VMEM((MLA_HEADS * tq, KV_LORA), F32)],
    )
    return pl.pallas_call(
        functools.partial(_mla_prompt_kernel, tq=tq, tk=tk),
        grid_spec=grid_spec,
        out_shape=jax.ShapeDtypeStruct((n_b, seq, LAT_WIDTH), BF16),
        compiler_params=pltpu.CompilerParams(dimension_semantics=("arbitrary", "arbitrary"),
                                             vmem_limit_bytes=VMEM_LIMIT),
        name="mla_prompt",
    )(qi_t, kj_t, q, kv)


def _mla_decode_kernel(pt_ref, q_ref, kvn_ref, *refs, n_group, n_new):
    ckv_refs = refs[:n_group]
    kr_refs = refs[n_group:2 * n_group]
    o_ref, m_sc, l_sc, acc_sc = refs[2 * n_group:]
    c = pl.program_id(1)
    rows = MLA_HEADS * n_new

    @pl.when(c == 0)
    def _():
        _softmax_init(m_sc, l_sc, acc_sc)

    q = q_ref[0].reshape(rows, QK_LAT).astype(BF16)
    lat = jnp.concatenate([r[0] for r in ckv_refs], axis=0).astype(BF16)
    rope = jnp.concatenate([r[0] for r in kr_refs], axis=0).astype(BF16)
    k = jnp.concatenate([lat, rope], axis=1)
    s = lax.dot_general(q, k, (((1,), (1,)), ((), ())), preferred_element_type=F32)
    _softmax_update(s, lat, m_sc, l_sc, acc_sc)

    @pl.when(c == pl.num_programs(1) - 1)
    def _():
        kn = jnp.concatenate([kvn_ref[0], jnp.zeros((LANES - n_new, QK_LAT), F32)], axis=0).astype(BF16)
        sn = lax.dot_general(q, kn, (((1,), (1,)), ((), ())), preferred_element_type=F32)
        tpos = lax.broadcasted_iota(jnp.int32, (rows, LANES), 0) & (n_new - 1)
        jpos = lax.broadcasted_iota(jnp.int32, (rows, LANES), 1)
        sn = jnp.where(jpos <= tpos, sn, -jnp.inf)
        _softmax_update(sn, kn[:, :KV_LORA], m_sc, l_sc, acc_sc)
        inv_l = 1.0 / l_sc[...]
        for h in range(MLA_HEADS):
            r = slice(h * n_new, (h + 1) * n_new)
            o_ref[0, :, h * KV_LORA:(h + 1) * KV_LORA] = (acc_sc[r, :] * inv_l[r, :]).astype(o_ref.dtype)


def _mla_decode_call(q, kv_new, ckv_pool, kr_pool, page_table, *, n_group):
    n_b, _, n_new, _ = q.shape
    assert n_new & (n_new - 1) == 0 and n_new <= LANES
    n_pages = page_table.shape[1]
    page = ckv_pool.shape[1]
    n_chunks = n_pages // n_group
    pt_flat = page_table.reshape(-1)

    def pool_spec(width, i):
        return pl.BlockSpec((1, page, width),
                            lambda b, c, pt: (pt[b * n_pages + c * n_group + i], 0, 0))

    grid_spec = pltpu.PrefetchScalarGridSpec(
        num_scalar_prefetch=1,
        grid=(n_b, n_chunks),
        in_specs=[pl.BlockSpec((1, MLA_HEADS, n_new, QK_LAT), lambda b, c, pt: (b, 0, 0, 0)),
                  pl.BlockSpec((1, n_new, QK_LAT), lambda b, c, pt: (b, 0, 0))]
        + [pool_spec(KV_LORA, i) for i in range(n_group)]
        + [pool_spec(QK_ROPE, i) for i in range(n_group)],
        out_specs=pl.BlockSpec((1, n_new, LAT_WIDTH), lambda b, c, pt: (b, 0, 0)),
        scratch_shapes=[pltpu.VMEM((MLA_HEADS * n_new, 1), F32), pltpu.VMEM((MLA_HEADS * n_new, 1), F32),
                        pltpu.VMEM((MLA_HEADS * n_new, KV_LORA), F32)],
    )
    return pl.pallas_call(
        functools.partial(_mla_decode_kernel, n_group=n_group, n_new=n_new),
        grid_spec=grid_spec,
        out_shape=jax.ShapeDtypeStruct((n_b, n_new, LAT_WIDTH), BF16),
        compiler_params=pltpu.CompilerParams(dimension_semantics=("arbitrary", "arbitrary"),
                                             vmem_limit_bytes=VMEM_LIMIT),
        name="mla_decode",
    )(pt_flat, q, kv_new, *([ckv_pool] * n_group), *([kr_pool] * n_group))


def _gla_levels(chunk):
    levels = []
    h = chunk // 2
    while h >= 1:
        levels.append(h)
        h //= 2
    return levels


def _gla_decay_matrix(chunk):
    t = np.arange(chunk)[:, None]
    u = np.arange(chunk)[None, :]
    blocks = [(u <= t), (u > t)]
    for h in _gla_levels(chunk):
        off = t % (2 * h)
        mid = t - off + h
        upper = (off >= h) & (u >= mid) & (u <= t)
        lower = (off < h) & (u > t) & (u < mid)
        blocks.append(upper | lower)
    return np.concatenate(blocks, axis=0).astype(np.float32)


def _gla_chunk(qk, v, la, g_mat, gn, s_sc, o_ref, r0, *, chunk):
    levels = _gla_levels(chunk)
    la_hi = la.astype(BF16)
    la_lo = (la - la_hi.astype(F32)).astype(BF16)
    e_all = (jnp.dot(g_mat, la_hi, preferred_element_type=F32)
             + jnp.dot(g_mat, la_lo, preferred_element_type=F32))
    q = qk[:, :GLA_QK] * (GLA_DK ** -0.5)
    k = qk[:, GLA_QK:]
    e_cum = e_all[0:chunk]
    q_in = q * jnp.exp(e_cum)
    k_out = k * jnp.exp(e_all[chunk:2 * chunk])
    row = lax.broadcasted_iota(jnp.int32, (chunk, GLA_QK), 0)
    q_lv, k_lv = [q], [k]
    for li, h in enumerate(levels):
        w = jnp.exp(e_all[(2 + li) * chunk:(3 + li) * chunk])
        upper = (row & (2 * h - 1)) >= h
        q_lv.append(jnp.where(upper, q * w, 0.0))
        k_lv.append(jnp.where(upper, 0.0, k * w))
    trow = lax.broadcasted_iota(jnp.int32, (chunk, chunk), 0)
    tcol = lax.broadcasted_iota(jnp.int32, (chunk, chunk), 1)
    masks = [trow == tcol] + [(trow // (2 * h)) == (tcol // (2 * h)) for h in levels]
    lane = lax.broadcasted_iota(jnp.int32, (chunk, LANES), 1)
    decay_row = e_cum[chunk - 1:chunk, :]
    contract_last = (((1,), (1,)), ((), ()))
    contract_first = (((0,), (0,)), ((), ()))
    for p in range(GLA_HEADS // 2):
        ls = slice(p * LANES, (p + 1) * LANES)
        k_pair = [kl[:, ls].astype(BF16) for kl in k_lv]
        s_pair = s_sc[ls, :]
        decay = jnp.exp(jnp.transpose(jnp.broadcast_to(decay_row[:, ls], (LANES, LANES))))
        k_out_pair = k_out[:, ls].astype(BF16)
        new_rows = []
        for half in range(2):
            hd = 2 * p + half
            in_head = (lane // GLA_DK) == half
            v_h = v[:, hd * GLA_DV:(hd + 1) * GLA_DV]
            a = jnp.zeros((chunk, chunk), F32)
            for ql, kl, mk in zip(q_lv, k_pair, masks):
                qm = jnp.where(in_head, ql[:, ls], 0.0).astype(BF16)
                a = a + jnp.where(mk, lax.dot_general(qm, kl, contract_last,
                                                      preferred_element_type=F32), 0.0)
            qi = jnp.where(in_head, q_in[:, ls], 0.0).astype(BF16)
            o = (jnp.dot(qi, s_pair.astype(BF16), preferred_element_type=F32)
                 + jnp.dot(a.astype(BF16), v_h, preferred_element_type=F32))
            o_ref[pl.ds(r0, chunk), hd * GLA_DV:(hd + 1) * GLA_DV] = (_rms(o) * gn).astype(o_ref.dtype)
            upd = lax.dot_general(k_out_pair, v_h, contract_first, preferred_element_type=F32)
            hr = slice(half * GLA_DK, (half + 1) * GLA_DK)
            new_rows.append(decay[hr, :] * s_pair[hr, :] + upd[hr, :])
        s_sc[ls, :] = jnp.concatenate(new_rows, axis=0)


def _gla_kernel(*refs, chunk, n_sub, has_init):
    if has_init:
        qk_ref, v_ref, la_ref, g_ref, gn_ref, s0_ref, o_ref, st_ref, s_sc = refs
    else:
        qk_ref, v_ref, la_ref, g_ref, gn_ref, o_ref, st_ref, s_sc = refs
    j = pl.program_id(1)

    @pl.when(j == 0)
    def _():
        s_sc[...] = s0_ref[0] if has_init else jnp.zeros(s_sc.shape, F32)

    g_mat = g_ref[...]
    gn = gn_ref[...]

    def body(c, carry):
        r0 = pl.multiple_of(c * chunk, chunk)
        _gla_chunk(qk_ref[pl.ds(r0, chunk), :], v_ref[pl.ds(r0, chunk), :], la_ref[pl.ds(r0, chunk), :],
                   g_mat, gn, s_sc, o_ref, r0, chunk=chunk)
        return carry

    lax.fori_loop(0, n_sub, body, 0)

    @pl.when(j == pl.num_programs(1) - 1)
    def _():
        st_ref[0] = s_sc[...]


def _gla_call(qk, v, la, gn, s0, *, n_seq, seq_len, rows, chunk):
    per = seq_len // rows
    g_mat = jnp.asarray(_gla_decay_matrix(chunk), BF16)
    tok = lambda width: pl.BlockSpec((rows, width), lambda b, j: (b * per + j, 0))
    full = lambda a: pl.BlockSpec(a.shape, lambda b, j: (0,) * a.ndim)
    state = pl.BlockSpec((1, GLA_QK, GLA_DV), lambda b, j: (b, 0, 0))
    has_init = s0 is not None
    args = [qk, v, la, g_mat, gn] + ([s0] if has_init else [])
    in_specs = [tok(2 * GLA_QK), tok(GLA_WIDTH), tok(GLA_QK), full(g_mat), full(gn)] + ([state] if has_init else [])
    return pl.pallas_call(
        functools.partial(_gla_kernel, chunk=chunk, n_sub=rows // chunk, has_init=has_init),
        grid=(n_seq, per),
        in_specs=in_specs,
        out_specs=(tok(GLA_WIDTH), state),
        out_shape=(jax.ShapeDtypeStruct((n_seq * seq_len, GLA_WIDTH), BF16),
                   jax.ShapeDtypeStruct((n_seq, GLA_QK, GLA_DV), F32)),
        scratch_shapes=[pltpu.VMEM((GLA_QK, GLA_DV), F32)],
        compiler_params=pltpu.CompilerParams(dimension_semantics=("arbitrary", "arbitrary"),
                                             vmem_limit_bytes=VMEM_LIMIT),
        name="gla",
    )(*args)


def _out_kernel(x_ref, olat_ref, gm_ref, ogla_ref, gg_ref, p_ref, wuv_ref, wout_ref, wpg_ref, wpp_ref,
                gfin_ref, y_ref):
    o_mla = jnp.dot(olat_ref[...], wuv_ref[...], preferred_element_type=F32) * gm_ref[...].astype(F32)
    o_gla = ogla_ref[...].astype(F32) * gg_ref[...].astype(F32)
    mix = jnp.concatenate([o_mla, o_gla], axis=1).astype(BF16)
    h = x_ref[...] + jnp.dot(mix, wout_ref[...], preferred_element_type=F32)
    gate = jnp.dot(h.astype(BF16), wpg_ref[...], preferred_element_type=F32)
    gate = 1.0 / (1.0 + jnp.exp(-gate))
    emb = jnp.dot(p_ref[...].astype(BF16), wpp_ref[...], preferred_element_type=F32)
    h = h + gate * emb
    y_ref[...] = _rms(h) * gfin_ref[...]


def _out_call(x2d, o_lat, gm, o_gla, gg, p2d, w, *, ts):
    n_tok = x2d.shape[0]
    tok = lambda width: pl.BlockSpec((ts, width), lambda i: (i, 0))
    full = lambda a: pl.BlockSpec(a.shape, lambda i: (0,) * a.ndim)
    weights = (w["w_uv_bd"], w["w_out"], w["w_pg"], w["w_pp"], w["g_final"])
    return pl.pallas_call(
        _out_kernel,
        grid=(n_tok // ts,),
        in_specs=[tok(D_MODEL), tok(LAT_WIDTH), tok(MLA_WIDTH), tok(GLA_WIDTH), tok(GLA_WIDTH), tok(P_DIM)]
        + [full(a) for a in weights],
        out_specs=tok(D_MODEL),
        out_shape=jax.ShapeDtypeStruct((n_tok, D_MODEL), F32),
        compiler_params=pltpu.CompilerParams(dimension_semantics=("arbitrary",),
                                             vmem_limit_bytes=VMEM_LIMIT),
        name="out",
    )(x2d, o_lat, gm, o_gla, gg, p2d, *weights)


def _prep_weights(g_mix_norm, w_in, g_qnorm, w_qup, g_kvnorm, w_uk, w_uv, w_gla_a2, b_gla_a, g_gla_onorm,
                  w_out, w_ple_gate, w_ple_proj, g_final):
    o_kv = Q_LORA
    o_gm = o_kv + KV_LORA + QK_ROPE
    o_gq = o_gm + MLA_WIDTH
    o_gk = o_gq + GLA_QK
    o_gv = o_gk + GLA_QK
    o_a = o_gv + GLA_WIDTH
    o_gg = o_a + GLA_GATE_RANK
    pad = jnp.zeros((D_MODEL, 512 - (Q_LORA + KV_LORA + QK_ROPE + GLA_GATE_RANK)), F32)
    w_all = jnp.concatenate([
        w_in[:, 0:o_gm], w_in[:, o_a:o_gg], pad,
        w_in[:, o_gm:o_gq],
        w_in[:, o_gq:o_gv],
        w_in[:, o_gv:o_a],
        w_in[:, o_gg:],
    ], axis=1).astype(BF16)
    assert w_all.shape == (D_MODEL, Z_WIDTH)
    w_qup3 = w_qup.reshape(Q_LORA, MLA_HEADS, QK_NOPE + QK_ROPE)
    w_qup_p = jnp.concatenate([w_qup3[:, :, :QK_NOPE].reshape(Q_LORA, MLA_HEADS * QK_NOPE),
                               w_qup3[:, :, QK_NOPE:].reshape(Q_LORA, MLA_HEADS * QK_ROPE)], axis=1).astype(BF16)
    eye = jnp.eye(MLA_HEADS, dtype=F32)
    w_uk_bd = (eye[:, None, :, None] * jnp.transpose(w_uk, (1, 2, 0))[:, :, None, :]
               ).reshape(MLA_HEADS * QK_NOPE, LAT_WIDTH).astype(BF16)
    w_uv_bd = (eye[:, None, :, None] * jnp.transpose(w_uv, (1, 0, 2))[:, :, None, :]
               ).reshape(LAT_WIDTH, MLA_WIDTH).astype(BF16)
    w_a2 = jnp.zeros((LANES, GLA_QK), F32).at[QK_ROPE:QK_ROPE + GLA_GATE_RANK].set(w_gla_a2).astype(BF16)
    row = lambda a: a.reshape(1, -1).astype(F32)
    return dict(g_mix=row(g_mix_norm), w_all=w_all, g_qn=row(g_qnorm), w_qup=w_qup_p, w_uk_bd=w_uk_bd,
                g_kvn=row(g_kvnorm), w_a2=w_a2, b_a=row(b_gla_a), g_gla=row(g_gla_onorm),
                w_uv_bd=w_uv_bd, w_out=w_out.astype(BF16), w_pg=w_ple_gate.astype(BF16),
                w_pp=w_ple_proj.astype(BF16), g_final=row(g_final))


def _rope_tables(pos):
    inv = 1.0 / (ROPE_THETA ** (jnp.arange(0, QK_ROPE, 2, dtype=F32) / QK_ROPE))
    ang = pos[:, None] * inv[None, :]
    cos, sin = jnp.cos(ang), jnp.sin(ang)
    reps = LANES // QK_ROPE
    return (jnp.tile(jnp.concatenate([cos, cos], axis=1), (1, reps)),
            jnp.tile(jnp.concatenate([-sin, sin], axis=1), (1, reps)))


def _pick_tile(n, pref):
    t = min(n, pref)
    while n % t:
        t //= 2
    return t


def kernel(x_prompt, x_sample, p_prompt, p_sample, cache_ckv, cache_krope, state_gla, page_table, g_mix_norm, w_in, g_qnorm, w_qup, g_kvnorm, w_uk, w_uv, w_gla_a2, b_gla_a, g_gla_onorm, w_out, w_ple_gate, w_ple_proj, g_final):
    n_b, seq, _ = x_prompt.shape
    n_d, n_new, _ = x_sample.shape
    assert w_in.shape[0] == 1, "single-layer stack"
    page = cache_ckv.shape[2]
    past_len = page_table.shape[1] * page
    w = _prep_weights(g_mix_norm[0], w_in[0], g_qnorm[0], w_qup[0], g_kvnorm[0], w_uk[0], w_uv[0],
                      w_gla_a2[0], b_gla_a[0], g_gla_onorm[0], w_out[0], w_ple_gate[0], w_ple_proj[0], g_final)

    ts_p = _pick_tile(seq, 512)
    cos_p, sin_p = _rope_tables(jnp.arange(seq, dtype=F32))
    xp = x_prompt.reshape(n_b * seq, D_MODEL)
    (q_p, kv_p, ckv_p, kr_p, gm_p, gqk_p, gv_p, la_p, gg_p) = _proj_call(
        xp, cos_p, sin_p, w, n_seq=n_b, seq_len=seq, ts=ts_p, q_dtype=BF16, table_period=seq // ts_p)
    tq = _pick_tile(seq, 256)
    tk = _pick_tile(seq, 512)
    olat_p = _mla_prompt_call(q_p, kv_p.reshape(n_b, seq, QK_LAT), tq=tq, tk=tk)
    chunk_p = min(GLA_CHUNK, seq)
    ogla_p, st_p = _gla_call(gqk_p, gv_p, la_p, w["g_gla"], None, n_seq=n_b, seq_len=seq,
                             rows=_pick_tile(seq, 512), chunk=chunk_p)
    y_p = _out_call(xp, olat_p.reshape(n_b * seq, LAT_WIDTH), gm_p, ogla_p, gg_p,
                    p_prompt[0].reshape(n_b * seq, P_DIM), w, ts=ts_p)

    n_tok_s = n_d * n_new
    ts_s = _pick_tile(n_tok_s, 512)
    cos_s, sin_s = _rope_tables(past_len + jnp.arange(n_new, dtype=F32))
    cos_s = jnp.tile(cos_s, (ts_s // n_new, 1))
    sin_s = jnp.tile(sin_s, (ts_s // n_new, 1))
    xs = x_sample.reshape(n_tok_s, D_MODEL)
    (q_s, kv_s, ckv_s, kr_s, gm_s, gqk_s, gv_s, la_s, gg_s) = _proj_call(
        xs, cos_s, sin_s, w, n_seq=n_d, seq_len=n_new, ts=ts_s, q_dtype=F32, table_period=1)
    n_group = _pick_tile(page_table.shape[1], 16)
    olat_s = _mla_decode_call(q_s, kv_s.reshape(n_d, n_new, QK_LAT), cache_ckv[0], cache_krope[0],
                              page_table, n_group=n_group)
    chunk_s = GLA_CHUNK
    pad_tok = lambda a: jnp.pad(a.reshape(n_d, n_new, -1), ((0, 0), (0, chunk_s - n_new), (0, 0))
                                ).reshape(n_d * chunk_s, -1)
    ogla_s, st_s = _gla_call(pad_tok(gqk_s), pad_tok(gv_s), pad_tok(la_s), w["g_gla"],
                             state_gla[0].reshape(n_d, GLA_QK, GLA_DV),
                             n_seq=n_d, seq_len=chunk_s, rows=chunk_s, chunk=chunk_s)
    ogla_s = ogla_s.reshape(n_d, chunk_s, GLA_WIDTH)[:, :n_new].reshape(n_tok_s, GLA_WIDTH)
    y_s = _out_call(xs, olat_s.reshape(n_tok_s, LAT_WIDTH), gm_s, ogla_s, gg_s,
                    p_sample[0].reshape(n_tok_s, P_DIM), w, ts=ts_s)

    return (y_p.reshape(n_b, seq, D_MODEL),
            y_s.reshape(n_d, n_new, D_MODEL),
            ckv_p.reshape(1, n_b, seq, KV_LORA),
            kr_p.reshape(1, n_b, seq, QK_ROPE),
            st_p.reshape(1, n_b, GLA_HEADS, GLA_DK, GLA_DV),
            ckv_s.reshape(1, n_d, n_new, KV_LORA),
            kr_s.reshape(1, n_d, n_new, QK_ROPE),
            st_s.reshape(1, n_d, GLA_HEADS, GLA_DK, GLA_DV))
```

```python
import functools

import numpy as np
import jax
import jax.numpy as jnp
from jax import lax
from jax.experimental import pallas as pl
from jax.experimental.pallas import tpu as pltpu

F32 = jnp.float32
BF16 = jnp.bfloat16

D_MODEL = 1024
MLA_HEADS = 8
QK_NOPE = 64
QK_ROPE = 32
V_HEAD = 64
KV_LORA = 128
Q_LORA = 256
MLA_WIDTH = MLA_HEADS * V_HEAD
GLA_HEADS = 4
GLA_DK = 64
GLA_DV = 128
GLA_WIDTH = GLA_HEADS * GLA_DV
GLA_QK = GLA_HEADS * GLA_DK
GLA_GATE_RANK = 16
GLA_TAU = 16.0
P_DIM = 256
NORM_EPS = 1e-6
ROPE_THETA = 10000.0
QK_LAT = KV_LORA + QK_ROPE
LAT_WIDTH = MLA_HEADS * KV_LORA
Z_WIDTH = 2560
GLA_CHUNK = 64
LOG2_E = 1.4426950408889634
GLA_UNROLL = 4
DMA_LOOP_UNROLL = 8

LANES = 128
VMEM_LIMIT = 56 * 1024 * 1024


def _rms(x):
    return x * lax.rsqrt(jnp.mean(x * x, axis=-1, keepdims=True) + NORM_EPS)


def _silu(x):
    return x * (1.0 / (1.0 + jnp.exp(-x)))


def _swap16(x):
    n = x.shape[-1]
    lane = lax.broadcasted_iota(jnp.int32, x.shape, x.ndim - 1)
    fwd = pltpu.roll(x, n - 16, axis=x.ndim - 1)
    bwd = pltpu.roll(x, 16, axis=x.ndim - 1)
    return jnp.where((lane & 31) < 16, fwd, bwd)


def _proj_kernel(x_ref, cos_ref, sin_ref, gmix_ref, wall_ref, gqn_ref, wqup_ref, wukbd_ref,
                 gkvn_ref, wa2_ref, ba_ref, eye_ref,
                 q_out, kv_out, ckv_out, kr_out, gm_out, gqk_out, gv_out, la_out, gg_out, ct_out=None,
                 *, scale, tq):
    x = x_ref[...]
    xn = (_rms(x) * gmix_ref[...]).astype(BF16)
    z = jnp.dot(xn, wall_ref[...], preferred_element_type=F32)
    cos = cos_ref[...]
    sin = sin_ref[...]

    qn = (_rms(z[:, 0:Q_LORA]) * gqn_ref[...]).astype(BF16)
    q = jnp.dot(qn, wqup_ref[...], preferred_element_type=F32)
    nope_w = MLA_HEADS * QK_NOPE
    q_lat = jnp.dot(q[:, :nope_w].astype(BF16), wukbd_ref[...],
                    preferred_element_type=F32) * scale
    qr = q[:, nope_w:]
    cos2 = jnp.concatenate([cos, cos], axis=1)
    sin2 = jnp.concatenate([sin, sin], axis=1)
    qr = (qr * cos2 + _swap16(qr) * sin2) * scale
    contract_last = (((1,), (1,)), ((), ()))
    for h in range(MLA_HEADS):
        q_h = jnp.concatenate([q_lat[:, h * KV_LORA:(h + 1) * KV_LORA],
                               qr[:, h * QK_ROPE:(h + 1) * QK_ROPE]], axis=1)
        if tq is None:
            nb, _, lb, _ = q_out.shape
            q_out[:, h, :, :] = q_h.reshape(nb, lb, QK_LAT).astype(q_out.dtype)
        else:
            q_t = lax.dot_general(eye_ref[...], q_h.astype(BF16), contract_last,
                                  preferred_element_type=F32).astype(q_out.dtype)
            for j in range(q_out.shape[1]):
                q_out[0, j, :, h * tq:(h + 1) * tq] = q_t[:, j * tq:(j + 1) * tq]

    ckv = _rms(z[:, 256:384]) * gkvn_ref[...]
    ckv_out[...] = ckv
    blk = z[:, 384:512]
    kr = (blk * cos + _swap16(blk) * sin)[:, :QK_ROPE]
    kr_out[...] = kr
    kv_out[:, 0:KV_LORA] = ckv.astype(kv_out.dtype)
    kv_out[:, KV_LORA:QK_LAT] = kr.astype(kv_out.dtype)
    if tq is not None:
        ct_out[0] = lax.dot_general(eye_ref[0:KV_LORA, 0:KV_LORA], ckv.astype(BF16), contract_last,
                                    preferred_element_type=F32).astype(ct_out.dtype)

    a_pre = jnp.dot(blk.astype(BF16), wa2_ref[...], preferred_element_type=F32) + ba_ref[...]
    la_out[...] = (jnp.minimum(a_pre, 0.0) - jnp.log1p(jnp.exp(-jnp.abs(a_pre)))) * (1.0 / GLA_TAU)

    gm_out[...] = _silu(z[:, 512:1024]).astype(gm_out.dtype)
    gqk_out[...] = z[:, 1024:1536]
    gv_out[...] = z[:, 1536:2048].astype(gv_out.dtype)
    gg_out[...] = _silu(z[:, 2048:2560]).astype(gg_out.dtype)


def _proj_call(x2d, cos_t, sin_t, w, *, n_seq, seq_len, ts, q_dtype, table_period, tq=None):
    n_tok = x2d.shape[0]
    n_tiles = n_tok // ts
    tok = lambda width: pl.BlockSpec((ts, width), lambda i: (i, 0))
    full = lambda a: pl.BlockSpec(a.shape, lambda i: (0,) * a.ndim)
    tab = pl.BlockSpec((ts, LANES), lambda i: (i % table_period, 0))
    if tq is None:
        assert ts % seq_len == 0
        q_shape = jax.ShapeDtypeStruct((n_seq, MLA_HEADS, seq_len, QK_LAT), q_dtype)
        q_spec = pl.BlockSpec((ts // seq_len, MLA_HEADS, seq_len, QK_LAT), lambda i: (i, 0, 0, 0))
        extra_shape, extra_spec = (), ()
    else:
        assert seq_len % ts == 0 and ts % tq == 0
        per = seq_len // ts
        q_shape = jax.ShapeDtypeStruct((n_seq, seq_len // tq, QK_LAT, MLA_HEADS * tq), q_dtype)
        q_spec = pl.BlockSpec((1, ts // tq, QK_LAT, MLA_HEADS * tq), lambda i: (i // per, i % per, 0, 0))
        extra_shape = (jax.ShapeDtypeStruct((n_seq, KV_LORA, seq_len), q_dtype),)
        extra_spec = (pl.BlockSpec((1, KV_LORA, ts), lambda i: (i // per, 0, i % per)),)
    out_shape = (
        q_shape,
        jax.ShapeDtypeStruct((n_tok, QK_LAT), q_dtype),
        jax.ShapeDtypeStruct((n_tok, KV_LORA), F32),
        jax.ShapeDtypeStruct((n_tok, QK_ROPE), F32),
        jax.ShapeDtypeStruct((n_tok, MLA_WIDTH), BF16),
        jax.ShapeDtypeStruct((n_tok, 2 * GLA_QK), F32),
        jax.ShapeDtypeStruct((n_tok, GLA_WIDTH), BF16),
        jax.ShapeDtypeStruct((n_tok, GLA_QK), F32),
        jax.ShapeDtypeStruct((n_tok, GLA_WIDTH), BF16),
    ) + extra_shape
    out_specs = (
        q_spec,
        tok(QK_LAT), tok(KV_LORA), tok(QK_ROPE), tok(MLA_WIDTH), tok(2 * GLA_QK),
        tok(GLA_WIDTH), tok(GLA_QK), tok(GLA_WIDTH),
    ) + extra_spec
    weights = (w["g_mix"], w["w_all"], w["g_qn"], w["w_qup"], w["w_uk_bd"], w["g_kvn"],
               w["w_a2"], w["b_a"], w["eye"])
    return pl.pallas_call(
        functools.partial(_proj_kernel, scale=(QK_NOPE + QK_ROPE) ** -0.5 * LOG2_E, tq=tq),
        grid=(n_tiles,),
        in_specs=[tok(D_MODEL), tab, tab] + [full(a) for a in weights],
        out_specs=out_specs,
        out_shape=out_shape,
        compiler_params=pltpu.CompilerParams(dimension_semantics=("arbitrary",),
                                             vmem_limit_bytes=VMEM_LIMIT),
        name="proj",
    )(x2d, cos_t, sin_t, *weights)


def _mla_prompt_kernel(qi_ref, kj_ref, qt_ref, kv_ref, ct_ref, o_ref, m_sc, l_sc, acc_sc, *, tq, tk, cw):
    step = pl.program_id(1)
    qi = qi_ref[step]
    kj = kj_ref[step]
    last_kj = (qi * tq + (tq - 1)) // tk
    n_cols = MLA_HEADS * tq

    @pl.when(kj == 0)
    def _():
        m_sc[...] = jnp.full(m_sc.shape, -jnp.inf, F32)
        l_sc[...] = jnp.zeros(l_sc.shape, F32)
        acc_sc[...] = jnp.zeros(acc_sc.shape, F32)

    def process(masked):
        kv = kv_ref[0]
        ct = ct_ref[0]
        for c in range(n_cols // cw):
            cols = slice(c * cw, (c + 1) * cw)
            s = jnp.dot(kv, qt_ref[0, 0, :, cols], preferred_element_type=F32)
            if masked:
                kpos = kj * tk + lax.broadcasted_iota(jnp.int32, (tk, cw), 0)
                qpos = qi * tq + ((c * cw + lax.broadcasted_iota(jnp.int32, (tk, cw), 1)) & (tq - 1))
                s = jnp.where(kpos <= qpos, s, -jnp.inf)
            m_prev = m_sc[:, cols]
            m_new = jnp.maximum(m_prev, jnp.max(s, axis=0, keepdims=True))
            alpha = jnp.exp2(m_prev - m_new)
            p = jnp.exp2(s - m_new)
            l_sc[:, cols] = alpha * l_sc[:, cols] + jnp.sum(p, axis=0, keepdims=True)
            acc_sc[:, cols] = alpha * acc_sc[:, cols] + jnp.dot(ct, p.astype(BF16), preferred_element_type=F32)
            m_sc[:, cols] = m_new

    @pl.when(kj < last_kj)
    def _():
        process(False)

    @pl.when(kj == last_kj)
    def _():
        process(True)
        for h in range(MLA_HEADS):
            cols = slice(h * tq, (h + 1) * tq)
            o_t = acc_sc[:, cols] * (1.0 / l_sc[:, cols])
            o_ref[0, :, h * KV_LORA:(h + 1) * KV_LORA] = jnp.transpose(o_t).astype(o_ref.dtype)


def _mla_prompt_call(qt, kv, ct, *, tq, tk):
    n_b, n_qb, _, n_cols = qt.shape
    seq = n_qb * tq
    assert tq & (tq - 1) == 0
    qi_l, kj_l = [], []
    for i in range(n_qb):
        for j in range((i * tq + tq - 1) // tk + 1):
            qi_l.append(i)
            kj_l.append(j)
    qi_t = jnp.asarray(np.array(qi_l, np.int32))
    kj_t = jnp.asarray(np.array(kj_l, np.int32))
    grid_spec = pltpu.PrefetchScalarGridSpec(
        num_scalar_prefetch=2,
        grid=(n_b, len(qi_l)),
        in_specs=[
            pl.BlockSpec((1, 1, QK_LAT, n_cols), lambda b, s, qi, kj: (b, qi[s], 0, 0)),
            pl.BlockSpec((1, tk, QK_LAT), lambda b, s, qi, kj: (b, kj[s], 0)),
            pl.BlockSpec((1, KV_LORA, tk), lambda b, s, qi, kj: (b, 0, kj[s])),
        ],
        out_specs=pl.BlockSpec((1, tq, LAT_WIDTH), lambda b, s, qi, kj: (b, qi[s], 0)),
        scratch_shapes=[pltpu.VMEM((1, n_cols), F32), pltpu.VMEM((1, n_cols), F32),
                        pltpu.VMEM((KV_LORA, n_cols), F32)],
    )
    return pl.pallas_call(
        functools.partial(_mla_prompt_kernel, tq=tq, tk=tk, cw=min(n_cols, 8 * LANES)),
        grid_spec=grid_spec,
        out_shape=jax.ShapeDtypeStruct((n_b, seq, LAT_WIDTH), BF16),
        compiler_params=pltpu.CompilerParams(dimension_semantics=("arbitrary", "arbitrary"),
                                             vmem_limit_bytes=VMEM_LIMIT),
        name="mla_prompt",
    )(qi_t, kj_t, qt, kv, ct)


def _softmax_update(s, v, m_sc, l_sc, acc_sc):
    m_prev = m_sc[...]
    m_new = jnp.maximum(m_prev, jnp.max(s, axis=1, keepdims=True))
    alpha = jnp.exp2(m_prev - m_new)
    p = jnp.exp2(s - m_new)
    l_sc[...] = alpha * l_sc[...] + jnp.sum(p, axis=1, keepdims=True)
    acc_sc[...] = alpha * acc_sc[...] + jnp.dot(p.astype(BF16), v, preferred_element_type=F32)
    m_sc[...] = m_new


def _mla_decode_kernel(pt_ref, q_ref, kvn_ref, lat_hbm, krt_hbm, o_ref,
                       lat_buf, krt_buf, sems, m_sc, l_sc, acc_sc, *, n_pages, page, n_new, kc):
    b = pl.program_id(0)
    slot = b % 2
    rows = MLA_HEADS * n_new

    def page_copies(seq, slot_, i):
        pg = pt_ref[seq * n_pages + i]
        off = pl.multiple_of(i * page, page)
        return (pltpu.make_async_copy(lat_hbm.at[pg], lat_buf.at[slot_, pl.ds(off, page), :], sems.at[0, slot_]),
                pltpu.make_async_copy(krt_hbm.at[pg], krt_buf.at[slot_, :, pl.ds(off, page)], sems.at[1, slot_]))

    def start_seq(seq, slot_):
        def body(i, carry):
            for cp in page_copies(seq, slot_, i):
                cp.start()
            return carry
        lax.fori_loop(0, n_pages, body, 0, unroll=DMA_LOOP_UNROLL)

    @pl.when(b == 0)
    def _():
        start_seq(0, 0)

    @pl.when(b + 1 < pl.num_programs(0))
    def _():
        start_seq(b + 1, 1 - slot)

    def wait_body(i, carry):
        for cp in page_copies(b, slot, i):
            cp.wait()
        return carry
    lax.fori_loop(0, n_pages, wait_body, 0, unroll=DMA_LOOP_UNROLL)

    m_sc[...] = jnp.full(m_sc.shape, -jnp.inf, F32)
    l_sc[...] = jnp.zeros(l_sc.shape, F32)
    acc_sc[...] = jnp.zeros(acc_sc.shape, F32)
    q = q_ref[0].reshape(rows, QK_LAT).astype(BF16)
    q_lat = q[:, :KV_LORA]
    q_rope = q[:, KV_LORA:]
    contract_last = (((1,), (1,)), ((), ()))
    for c in range(n_pages * page // kc):
        lat = lat_buf[slot, c * kc:(c + 1) * kc, :].astype(BF16)
        krt = krt_buf[slot, :, c * kc:(c + 1) * kc].astype(BF16)
        s = (lax.dot_general(q_lat, lat, contract_last, preferred_element_type=F32)
             + jnp.dot(q_rope, krt, preferred_element_type=F32))
        _softmax_update(s, lat, m_sc, l_sc, acc_sc)

    kn = jnp.concatenate([kvn_ref[0], jnp.zeros((LANES - n_new, QK_LAT), F32)], axis=0).astype(BF16)
    sn = lax.dot_general(q, kn, contract_last, preferred_element_type=F32)
    tpos = lax.broadcasted_iota(jnp.int32, (rows, LANES), 0) & (n_new - 1)
    jpos = lax.broadcasted_iota(jnp.int32, (rows, LANES), 1)
    sn = jnp.where(jpos <= tpos, sn, -jnp.inf)
    _softmax_update(sn, kn[:, :KV_LORA], m_sc, l_sc, acc_sc)
    inv_l = 1.0 / l_sc[...]
    for h in range(MLA_HEADS):
        r = slice(h * n_new, (h + 1) * n_new)
        o_ref[0, :, h * KV_LORA:(h + 1) * KV_LORA] = (acc_sc[r, :] * inv_l[r, :]).astype(o_ref.dtype)


def _mla_decode_call(q, kv_new, lat_pool, krt_pool, page_table):
    n_b, _, n_new, _ = q.shape
    assert n_new & (n_new - 1) == 0 and n_new <= LANES
    n_pages = page_table.shape[1]
    page = lat_pool.shape[1]
    past = n_pages * page
    kc = _pick_tile(past, 4096)
    grid_spec = pltpu.PrefetchScalarGridSpec(
        num_scalar_prefetch=1,
        grid=(n_b,),
        in_specs=[pl.BlockSpec((1, MLA_HEADS, n_new, QK_LAT), lambda b, pt: (b, 0, 0, 0)),
                  pl.BlockSpec((1, n_new, QK_LAT), lambda b, pt: (b, 0, 0)),
                  pl.BlockSpec(memory_space=pl.ANY),
                  pl.BlockSpec(memory_space=pl.ANY)],
        out_specs=pl.BlockSpec((1, n_new, LAT_WIDTH), lambda b, pt: (b, 0, 0)),
        scratch_shapes=[pltpu.VMEM((2, past, KV_LORA), F32), pltpu.VMEM((2, QK_ROPE, past), F32),
                        pltpu.SemaphoreType.DMA((2, 2)),
                        pltpu.VMEM((MLA_HEADS * n_new, 1), F32), pltpu.VMEM((MLA_HEADS * n_new, 1), F32),
                        pltpu.VMEM((MLA_HEADS * n_new, KV_LORA), F32)],
    )
    return pl.pallas_call(
        functools.partial(_mla_decode_kernel, n_pages=n_pages, page=page, n_new=n_new, kc=kc),
        grid_spec=grid_spec,
        out_shape=jax.ShapeDtypeStruct((n_b, n_new, LAT_WIDTH), BF16),
        compiler_params=pltpu.CompilerParams(dimension_semantics=("arbitrary",),
                                             vmem_limit_bytes=VMEM_LIMIT),
        name="mla_decode",
    )(page_table.reshape(-1), q, kv_new, lat_pool, krt_pool)


def _gla_levels(chunk):
    levels = []
    h = chunk // 2
    while h >= 1:
        levels.append(h)
        h //= 2
    return levels


def _gla_decay_matrix(chunk):
    t = np.arange(chunk)[:, None]
    u = np.arange(chunk)[None, :]
    blocks = [(u <= t), (u > t)]
    for h in _gla_levels(chunk):
        off = t % (2 * h)
        mid = t - off + h
        upper = (off >= h) & (u >= mid) & (u <= t)
        lower = (off < h) & (u > t) & (u < mid)
        blocks.append(upper | lower)
    return np.concatenate(blocks, axis=0).astype(np.float32)


def _gla_chunk(qk, v, la, g_mat, gn, s_sc, o_ref, r0, *, chunk):
    levels = _gla_levels(chunk)
    la_hi = la.astype(BF16)
    la_lo = (la - la_hi.astype(F32)).astype(BF16)
    e_all = (jnp.dot(g_mat, la_hi, preferred_element_type=F32)
             + jnp.dot(g_mat, la_lo, preferred_element_type=F32))
    q = qk[:, :GLA_QK] * (GLA_DK ** -0.5)
    k = qk[:, GLA_QK:]
    e_cum = e_all[0:chunk]
    q_in = q * jnp.exp(e_cum)
    k_out = k * jnp.exp(e_all[chunk:2 * chunk])
    row = lax.broadcasted_iota(jnp.int32, (chunk, GLA_QK), 0)
    q_lv, k_lv = [q], [k]
    for li, h in enumerate(levels):
        w = jnp.exp(e_all[(2 + li) * chunk:(3 + li) * chunk])
        upper = (row & (2 * h - 1)) >= h
        q_lv.append(jnp.where(upper, q * w, 0.0))
        k_lv.append(jnp.where(upper, 0.0, k * w))
    trow = lax.broadcasted_iota(jnp.int32, (chunk, chunk), 0)
    tcol = lax.broadcasted_iota(jnp.int32, (chunk, chunk), 1)
    masks = [trow == tcol] + [(trow // (2 * h)) == (tcol // (2 * h)) for h in levels]
    lane = lax.broadcasted_iota(jnp.int32, (chunk, LANES), 1)
    decay_row = e_cum[chunk - 1:chunk, :]
    contract_last = (((1,), (1,)), ((), ()))
    contract_first = (((0,), (0,)), ((), ()))
    for p in range(GLA_HEADS // 2):
        ls = slice(p * LANES, (p + 1) * LANES)
        k_pair = [kl[:, ls].astype(BF16) for kl in k_lv]
        s_pair = s_sc[ls, :]
        decay = jnp.exp(jnp.transpose(jnp.broadcast_to(decay_row[:, ls], (LANES, LANES))))
        k_out_pair = k_out[:, ls].astype(BF16)
        new_rows = []
        for half in range(2):
            hd = 2 * p + half
            in_head = (lane // GLA_DK) == half
            v_h = v[:, hd * GLA_DV:(hd + 1) * GLA_DV]
            a = jnp.zeros((chunk, chunk), F32)
            for ql, kl, mk in zip(q_lv, k_pair, masks):
                qm = jnp.where(in_head, ql[:, ls], 0.0).astype(BF16)
                a = a + jnp.where(mk, lax.dot_general(qm, kl, contract_last,
                                                      preferred_element_type=F32), 0.0)
            qi = jnp.where(in_head, q_in[:, ls], 0.0).astype(BF16)
            o = (jnp.dot(qi, s_pair.astype(BF16), preferred_element_type=F32)
                 + jnp.dot(a.astype(BF16), v_h, preferred_element_type=F32))
            o_ref[pl.ds(r0, chunk), hd * GLA_DV:(hd + 1) * GLA_DV] = (_rms(o) * gn).astype(o_ref.dtype)
            upd = lax.dot_general(k_out_pair, v_h, contract_first, preferred_element_type=F32)
            hr = slice(half * GLA_DK, (half + 1) * GLA_DK)
            new_rows.append(decay[hr, :] * s_pair[hr, :] + upd[hr, :])
        s_sc[ls, :] = jnp.concatenate(new_rows, axis=0)


def _gla_kernel(*refs, chunk, n_sub, has_init):
    if has_init:
        qk_ref, v_ref, la_ref, g_ref, gn_ref, s0_ref, o_ref, st_ref, s_sc = refs
    else:
        qk_ref, v_ref, la_ref, g_ref, gn_ref, o_ref, st_ref, s_sc = refs
    j = pl.program_id(1)

    @pl.when(j == 0)
    def _():
        s_sc[...] = s0_ref[0] if has_init else jnp.zeros(s_sc.shape, F32)

    g_mat = g_ref[...]
    gn = gn_ref[...]

    def body(c, carry):
        r0 = pl.multiple_of(c * chunk, chunk)
        _gla_chunk(qk_ref[pl.ds(r0, chunk), :], v_ref[pl.ds(r0, chunk), :], la_ref[pl.ds(r0, chunk), :],
                   g_mat, gn, s_sc, o_ref, r0, chunk=chunk)
        return carry

    lax.fori_loop(0, n_sub, body, 0, unroll=min(n_sub, GLA_UNROLL))

    @pl.when(j == pl.num_programs(1) - 1)
    def _():
        st_ref[0] = s_sc[...]


def _gla_call(qk, v, la, gn, s0, *, n_seq, seq_len, rows, chunk):
    per = seq_len // rows
    g_mat = jnp.asarray(_gla_decay_matrix(chunk), BF16)
    tok = lambda width: pl.BlockSpec((rows, width), lambda b, j: (b * per + j, 0))
    full = lambda a: pl.BlockSpec(a.shape, lambda b, j: (0,) * a.ndim)
    state = pl.BlockSpec((1, GLA_QK, GLA_DV), lambda b, j: (b, 0, 0))
    has_init = s0 is not None
    args = [qk, v, la, g_mat, gn] + ([s0] if has_init else [])
    in_specs = [tok(2 * GLA_QK), tok(GLA_WIDTH), tok(GLA_QK), full(g_mat), full(gn)] + ([state] if has_init else [])
    return pl.pallas_call(
        functools.partial(_gla_kernel, chunk=chunk, n_sub=rows // chunk, has_init=has_init),
        grid=(n_seq, per),
        in_specs=in_specs,
        out_specs=(tok(GLA_WIDTH), state),
        out_shape=(jax.ShapeDtypeStruct((n_seq * seq_len, GLA_WIDTH), BF16),
                   jax.ShapeDtypeStruct((n_seq, GLA_QK, GLA_DV), F32)),
        scratch_shapes=[pltpu.VMEM((GLA_QK, GLA_DV), F32)],
        compiler_params=pltpu.CompilerParams(dimension_semantics=("arbitrary", "arbitrary"),
                                             vmem_limit_bytes=VMEM_LIMIT),
        name="gla",
    )(*args)


def _out_kernel(x_ref, olat_ref, gm_ref, ogla_ref, gg_ref, p_ref, wuv_ref, wout_ref, wpg_ref, wpp_ref,
                gfin_ref, y_ref):
    o_mla = jnp.dot(olat_ref[...], wuv_ref[...], preferred_element_type=F32) * gm_ref[...].astype(F32)
    o_gla = ogla_ref[...].astype(F32) * gg_ref[...].astype(F32)
    mix = jnp.concatenate([o_mla, o_gla], axis=1).astype(BF16)
    h = x_ref[...] + jnp.dot(mix, wout_ref[...], preferred_element_type=F32)
    gate = jnp.dot(h.astype(BF16), wpg_ref[...], preferred_element_type=F32)
    gate = 1.0 / (1.0 + jnp.exp(-gate))
    emb = jnp.dot(p_ref[...].astype(BF16), wpp_ref[...], preferred_element_type=F32)
    h = h + gate * emb
    y_ref[...] = _rms(h) * gfin_ref[...]


def _out_call(x2d, o_lat, gm, o_gla, gg, p2d, w, *, ts):
    n_tok = x2d.shape[0]
    tok = lambda width: pl.BlockSpec((ts, width), lambda i: (i, 0))
    full = lambda a: pl.BlockSpec(a.shape, lambda i: (0,) * a.ndim)
    weights = (w["w_uv_bd"], w["w_out"], w["w_pg"], w["w_pp"], w["g_final"])
    return pl.pallas_call(
        _out_kernel,
        grid=(n_tok // ts,),
        in_specs=[tok(D_MODEL), tok(LAT_WIDTH), tok(MLA_WIDTH), tok(GLA_WIDTH), tok(GLA_WIDTH), tok(P_DIM)]
        + [full(a) for a in weights],
        out_specs=tok(D_MODEL),
        out_shape=jax.ShapeDtypeStruct((n_tok, D_MODEL), F32),
        compiler_params=pltpu.CompilerParams(dimension_semantics=("arbitrary",),
                                             vmem_limit_bytes=VMEM_LIMIT),
        name="out",
    )(x2d, o_lat, gm, o_gla, gg, p2d, *weights)


def _prep_weights(g_mix_norm, w_in, g_qnorm, w_qup, g_kvnorm, w_uk, w_uv, w_gla_a2, b_gla_a, g_gla_onorm,
                  w_out, w_ple_gate, w_ple_proj, g_final):
    o_kv = Q_LORA
    o_gm = o_kv + KV_LORA + QK_ROPE
    o_gq = o_gm + MLA_WIDTH
    o_gk = o_gq + GLA_QK
    o_gv = o_gk + GLA_QK
    o_a = o_gv + GLA_WIDTH
    o_gg = o_a + GLA_GATE_RANK
    pad = jnp.zeros((D_MODEL, 512 - (Q_LORA + KV_LORA + QK_ROPE + GLA_GATE_RANK)), F32)
    w_all = jnp.concatenate([
        w_in[:, 0:o_gm], w_in[:, o_a:o_gg], pad,
        w_in[:, o_gm:o_gq],
        w_in[:, o_gq:o_gv],
        w_in[:, o_gv:o_a],
        w_in[:, o_gg:],
    ], axis=1).astype(BF16)
    assert w_all.shape == (D_MODEL, Z_WIDTH)
    w_qup3 = w_qup.reshape(Q_LORA, MLA_HEADS, QK_NOPE + QK_ROPE)
    w_qup_p = jnp.concatenate([w_qup3[:, :, :QK_NOPE].reshape(Q_LORA, MLA_HEADS * QK_NOPE),
                               w_qup3[:, :, QK_NOPE:].reshape(Q_LORA, MLA_HEADS * QK_ROPE)], axis=1).astype(BF16)
    eye = jnp.eye(MLA_HEADS, dtype=F32)
    w_uk_bd = (eye[:, None, :, None] * jnp.transpose(w_uk, (1, 2, 0))[:, :, None, :]
               ).reshape(MLA_HEADS * QK_NOPE, LAT_WIDTH).astype(BF16)
    w_uv_bd = (eye[:, None, :, None] * jnp.transpose(w_uv, (1, 0, 2))[:, :, None, :]
               ).reshape(LAT_WIDTH, MLA_WIDTH).astype(BF16)
    w_a2 = jnp.zeros((LANES, GLA_QK), F32).at[QK_ROPE:QK_ROPE + GLA_GATE_RANK].set(w_gla_a2).astype(BF16)
    row = lambda a: a.reshape(1, -1).astype(F32)
    return dict(g_mix=row(g_mix_norm), w_all=w_all, g_qn=row(g_qnorm), w_qup=w_qup_p, w_uk_bd=w_uk_bd,
                g_kvn=row(g_kvnorm), w_a2=w_a2, b_a=row(b_gla_a), g_gla=row(g_gla_onorm),
                w_uv_bd=w_uv_bd, w_out=w_out.astype(BF16), w_pg=w_ple_gate.astype(BF16),
                w_pp=w_ple_proj.astype(BF16), g_final=row(g_final), eye=jnp.eye(QK_LAT, dtype=BF16))


def _rope_tables(pos):
    inv = 1.0 / (ROPE_THETA ** (jnp.arange(0, QK_ROPE, 2, dtype=F32) / QK_ROPE))
    ang = pos[:, None] * inv[None, :]
    cos, sin = jnp.cos(ang), jnp.sin(ang)
    reps = LANES // QK_ROPE
    return (jnp.tile(jnp.concatenate([cos, cos], axis=1), (1, reps)),
            jnp.tile(jnp.concatenate([-sin, sin], axis=1), (1, reps)))


def _pick_tile(n, pref):
    t = min(n, pref)
    while n % t:
        t //= 2
    return t


def kernel(x_prompt, x_sample, p_prompt, p_sample, cache_ckv, cache_krope, state_gla, page_table, g_mix_norm, w_in, g_qnorm, w_qup, g_kvnorm, w_uk, w_uv, w_gla_a2, b_gla_a, g_gla_onorm, w_out, w_ple_gate, w_ple_proj, g_final):
    n_b, seq, _ = x_prompt.shape
    n_d, n_new, _ = x_sample.shape
    assert w_in.shape[0] == 1, "single-layer stack"
    page = cache_ckv.shape[2]
    past_len = page_table.shape[1] * page
    w = _prep_weights(g_mix_norm[0], w_in[0], g_qnorm[0], w_qup[0], g_kvnorm[0], w_uk[0], w_uv[0],
                      w_gla_a2[0], b_gla_a[0], g_gla_onorm[0], w_out[0], w_ple_gate[0], w_ple_proj[0], g_final)

    ts_p = _pick_tile(seq, 512)
    cos_p, sin_p = _rope_tables(jnp.arange(seq, dtype=F32))
    xp = x_prompt.reshape(n_b * seq, D_MODEL)
    tq = _pick_tile(seq, 512)
    tk = _pick_tile(seq, 512)
    (qt_p, kv_p, ckv_p, kr_p, gm_p, gqk_p, gv_p, la_p, gg_p, ct_p) = _proj_call(
        xp, cos_p, sin_p, w, n_seq=n_b, seq_len=seq, ts=ts_p, q_dtype=BF16, table_period=seq // ts_p, tq=tq)
    olat_p = _mla_prompt_call(qt_p, kv_p.reshape(n_b, seq, QK_LAT), ct_p, tq=tq, tk=tk)
    chunk_p = min(GLA_CHUNK, seq)
    ogla_p, st_p = _gla_call(gqk_p, gv_p, la_p, w["g_gla"], None, n_seq=n_b, seq_len=seq,
                             rows=_pick_tile(seq, 512), chunk=chunk_p)
    y_p = _out_call(xp, olat_p.reshape(n_b * seq, LAT_WIDTH), gm_p, ogla_p, gg_p,
                    p_prompt[0].reshape(n_b * seq, P_DIM), w, ts=ts_p)

    n_tok_s = n_d * n_new
    ts_s = _pick_tile(n_tok_s, 512)
    cos_s, sin_s = _rope_tables(past_len + jnp.arange(n_new, dtype=F32))
    cos_s = jnp.tile(cos_s, (ts_s // n_new, 1))
    sin_s = jnp.tile(sin_s, (ts_s // n_new, 1))
    xs = x_sample.reshape(n_tok_s, D_MODEL)
    (q_s, kv_s, ckv_s, kr_s, gm_s, gqk_s, gv_s, la_s, gg_s) = _proj_call(
        xs, cos_s, sin_s, w, n_seq=n_d, seq_len=n_new, ts=ts_s, q_dtype=F32, table_period=1)
    olat_s = _mla_decode_call(q_s, kv_s.reshape(n_d, n_new, QK_LAT), cache_ckv[0],
                              jnp.swapaxes(cache_krope[0], 1, 2), page_table)
    chunk_s = GLA_CHUNK
    pad_tok = lambda a: jnp.pad(a.reshape(n_d, n_new, -1), ((0, 0), (0, chunk_s - n_new), (0, 0))
                                ).reshape(n_d * chunk_s, -1)
    ogla_s, st_s = _gla_call(pad_tok(gqk_s), pad_tok(gv_s), pad_tok(la_s), w["g_gla"],
                             state_gla[0].reshape(n_d, GLA_QK, GLA_DV),
                             n_seq=n_d, seq_len=chunk_s, rows=chunk_s, chunk=chunk_s)
    ogla_s = ogla_s.reshape(n_d, chunk_s, GLA_WIDTH)[:, :n_new].reshape(n_tok_s, GLA_WIDTH)
    y_s = _out_call(xs, olat_s.reshape(n_tok_s, LAT_WIDTH), gm_s, ogla_s, gg_s,
                    p_sample[0].reshape(n_tok_s, P_DIM), w, ts=ts_s)

    return (y_p.reshape(n_b, seq, D_MODEL),
            y_s.reshape(n_d, n_new, D_MODEL),
            ckv_p.reshape(1, n_b, seq, KV_LORA),
            kr_p.reshape(1, n_b, seq, QK_ROPE),
            st_p.reshape(1, n_b, GLA_HEADS, GLA_DK, GLA_DV),
            ckv_s.reshape(1, n_d, n_new, KV_LORA),
            kr_s.reshape(1, n_d, n_new, QK_ROPE),
            st_s.reshape(1, n_d, GLA_HEADS, GLA_DK, GLA_DV))
```

```python
import functools

import numpy as np
import jax
import jax.numpy as jnp
from jax import lax
from jax.experimental import pallas as pl
from jax.experimental.pallas import tpu as pltpu

F32 = jnp.float32
BF16 = jnp.bfloat16

D_MODEL = 1024
MLA_HEADS = 8
QK_NOPE = 64
QK_ROPE = 32
V_HEAD = 64
KV_LORA = 128
Q_LORA = 256
MLA_WIDTH = MLA_HEADS * V_HEAD
GLA_HEADS = 4
GLA_DK = 64
GLA_DV = 128
GLA_WIDTH = GLA_HEADS * GLA_DV
GLA_QK = GLA_HEADS * GLA_DK
GLA_GATE_RANK = 16
GLA_TAU = 16.0
P_DIM = 256
NORM_EPS = 1e-6
ROPE_THETA = 10000.0
QK_LAT = KV_LORA + QK_ROPE
LAT_WIDTH = MLA_HEADS * KV_LORA
Z_WIDTH = 2560
GLA_CHUNK = 64
LOG2_E = 1.4426950408889634
GLA_SAMPLE_CHUNK = 16
GLA_UNROLL = 4
MLA_CHUNK_COLS = 1024
ONES_ROWS = 16
DMA_LOOP_UNROLL = 8

LANES = 128
VMEM_LIMIT = 56 * 1024 * 1024


def _rms(x):
    return x * lax.rsqrt(jnp.mean(x * x, axis=-1, keepdims=True) + NORM_EPS)


def _silu(x):
    return x * (1.0 / (1.0 + jnp.exp(-x)))


def _swap16(x):
    n = x.shape[-1]
    lane = lax.broadcasted_iota(jnp.int32, x.shape, x.ndim - 1)
    fwd = pltpu.roll(x, n - 16, axis=x.ndim - 1)
    bwd = pltpu.roll(x, 16, axis=x.ndim - 1)
    return jnp.where((lane & 31) < 16, fwd, bwd)


def _proj_kernel(x_ref, cos_ref, sin_ref, gmix_ref, wall_ref, gqn_ref, wqup_ref, wukbd_ref,
                 gkvn_ref, wa2_ref, ba_ref,
                 q_out, kv_out, ckv_out, kr_out, gm_out, gqk_out, gv_out, la_out, gg_out, ct_out=None,
                 *, scale, tq):
    x = x_ref[...]
    xn = (_rms(x) * gmix_ref[...]).astype(BF16)
    z = jnp.dot(xn, wall_ref[...], preferred_element_type=F32)
    cos = cos_ref[...]
    sin = sin_ref[...]

    qn = (_rms(z[:, 0:Q_LORA]) * gqn_ref[...]).astype(BF16)
    q = jnp.dot(qn, wqup_ref[...], preferred_element_type=F32)
    nope_w = MLA_HEADS * QK_NOPE
    q_lat = jnp.dot(q[:, :nope_w].astype(BF16), wukbd_ref[...],
                    preferred_element_type=F32) * scale
    qr = q[:, nope_w:]
    cos2 = jnp.concatenate([cos, cos], axis=1)
    sin2 = jnp.concatenate([sin, sin], axis=1)
    qr = (qr * cos2 + _swap16(qr) * sin2) * scale
    if tq is None:
        nb, _, lb, _ = q_out.shape
        for h in range(MLA_HEADS):
            q_h = jnp.concatenate([q_lat[:, h * KV_LORA:(h + 1) * KV_LORA],
                                   qr[:, h * QK_ROPE:(h + 1) * QK_ROPE]], axis=1)
            q_out[:, h, :, :] = q_h.reshape(nb, lb, QK_LAT).astype(q_out.dtype)
    else:
        q_lat_t = jnp.transpose(q_lat).astype(q_out.dtype)
        qr_t = jnp.transpose(qr).astype(q_out.dtype)
        for h in range(MLA_HEADS):
            for j in range(q_out.shape[1]):
                tok = slice(j * tq, (j + 1) * tq)
                q_out[0, j, 0:KV_LORA, h * tq:(h + 1) * tq] = q_lat_t[h * KV_LORA:(h + 1) * KV_LORA, tok]
                q_out[0, j, KV_LORA:QK_LAT, h * tq:(h + 1) * tq] = qr_t[h * QK_ROPE:(h + 1) * QK_ROPE, tok]

    ckv = _rms(z[:, 256:384]) * gkvn_ref[...]
    ckv_out[...] = ckv
    blk = z[:, 384:512]
    kr = (blk * cos + _swap16(blk) * sin)[:, :QK_ROPE]
    kr_out[...] = kr
    kv_out[:, 0:KV_LORA] = ckv.astype(kv_out.dtype)
    kv_out[:, KV_LORA:QK_LAT] = kr.astype(kv_out.dtype)
    if tq is not None:
        ct_out[0] = jnp.transpose(ckv).astype(ct_out.dtype)

    a_pre = jnp.dot(blk.astype(BF16), wa2_ref[...], preferred_element_type=F32) + ba_ref[...]
    la_out[...] = (jnp.minimum(a_pre, 0.0) - jnp.log1p(jnp.exp(-jnp.abs(a_pre)))) * (1.0 / GLA_TAU)

    gm_out[...] = _silu(z[:, 512:1024]).astype(gm_out.dtype)
    gqk_out[...] = z[:, 1024:1536]
    gv_out[...] = z[:, 1536:2048].astype(gv_out.dtype)
    gg_out[...] = _silu(z[:, 2048:2560]).astype(gg_out.dtype)


def _proj_call(x2d, cos_t, sin_t, w, *, n_seq, seq_len, ts, q_dtype, table_period, tq=None):
    n_tok = x2d.shape[0]
    n_tiles = n_tok // ts
    tok = lambda width: pl.BlockSpec((ts, width), lambda i: (i, 0))
    full = lambda a: pl.BlockSpec(a.shape, lambda i: (0,) * a.ndim)
    tab = pl.BlockSpec((ts, LANES), lambda i: (i % table_period, 0))
    if tq is None:
        assert ts % seq_len == 0
        q_shape = jax.ShapeDtypeStruct((n_seq, MLA_HEADS, seq_len, QK_LAT), q_dtype)
        q_spec = pl.BlockSpec((ts // seq_len, MLA_HEADS, seq_len, QK_LAT), lambda i: (i, 0, 0, 0))
        extra_shape, extra_spec = (), ()
    else:
        assert seq_len % ts == 0 and ts % tq == 0
        per = seq_len // ts
        q_shape = jax.ShapeDtypeStruct((n_seq, seq_len // tq, QK_LAT, MLA_HEADS * tq), q_dtype)
        q_spec = pl.BlockSpec((1, ts // tq, QK_LAT, MLA_HEADS * tq), lambda i: (i // per, i % per, 0, 0))
        extra_shape = (jax.ShapeDtypeStruct((n_seq, KV_LORA, seq_len), q_dtype),)
        extra_spec = (pl.BlockSpec((1, KV_LORA, ts), lambda i: (i // per, 0, i % per)),)
    out_shape = (
        q_shape,
        jax.ShapeDtypeStruct((n_tok, QK_LAT), q_dtype),
        jax.ShapeDtypeStruct((n_tok, KV_LORA), F32),
        jax.ShapeDtypeStruct((n_tok, QK_ROPE), F32),
        jax.ShapeDtypeStruct((n_tok, MLA_WIDTH), BF16),
        jax.ShapeDtypeStruct((n_tok, 2 * GLA_QK), F32),
        jax.ShapeDtypeStruct((n_tok, GLA_WIDTH), BF16),
        jax.ShapeDtypeStruct((n_tok, GLA_QK), F32),
        jax.ShapeDtypeStruct((n_tok, GLA_WIDTH), BF16),
    ) + extra_shape
    out_specs = (
        q_spec,
        tok(QK_LAT), tok(KV_LORA), tok(QK_ROPE), tok(MLA_WIDTH), tok(2 * GLA_QK),
        tok(GLA_WIDTH), tok(GLA_QK), tok(GLA_WIDTH),
    ) + extra_spec
    weights = (w["g_mix"], w["w_all"], w["g_qn"], w["w_qup"], w["w_uk_bd"], w["g_kvn"],
               w["w_a2"], w["b_a"])
    return pl.pallas_call(
        functools.partial(_proj_kernel, scale=(QK_NOPE + QK_ROPE) ** -0.5 * LOG2_E, tq=tq),
        grid=(n_tiles,),
        in_specs=[tok(D_MODEL), tab, tab] + [full(a) for a in weights],
        out_specs=out_specs,
        out_shape=out_shape,
        compiler_params=pltpu.CompilerParams(dimension_semantics=("arbitrary",),
                                             vmem_limit_bytes=VMEM_LIMIT),
        name="proj",
    )(x2d, cos_t, sin_t, *weights)


def _mla_prompt_kernel(qi_ref, kj_ref, qt_ref, kv_ref, ct_ref, o_ref, m_sc, acc_sc, s_buf, p_buf, *, tq, tk, cw):
    step = pl.program_id(1)
    qi = qi_ref[step]
    kj = kj_ref[step]
    last_kj = (qi * tq + (tq - 1)) // tk
    n_cols = MLA_HEADS * tq

    @pl.when(kj == 0)
    def _():
        m_sc[...] = jnp.full(m_sc.shape, -jnp.inf, F32)
        acc_sc[...] = jnp.zeros(acc_sc.shape, F32)

    def process(masked):
        kv = kv_ref[0]
        ct = jnp.concatenate([ct_ref[0], jnp.ones((ONES_ROWS, tk), BF16)], axis=0)
        n_chunks = n_cols // cw

        def scores(c):
            cols = slice(c * cw, (c + 1) * cw)
            s = jnp.dot(kv, qt_ref[0, 0, :, cols], preferred_element_type=F32)
            if masked:
                kpos = kj * tk + lax.broadcasted_iota(jnp.int32, (tk, cw), 0)
                qpos = qi * tq + ((c * cw + lax.broadcasted_iota(jnp.int32, (tk, cw), 1)) & (tq - 1))
                s = jnp.where(kpos <= qpos, s, -jnp.inf)
            s_buf[c % 2] = s
            m_prev = m_sc[:, cols]
            m_new = jnp.maximum(m_prev, jnp.max(s, axis=0, keepdims=True))
            m_sc[:, cols] = m_new
            return m_new, jnp.exp2(m_prev - m_new)

        def probs(c, m_new):
            p_buf[c % 2] = jnp.exp2((s_buf[c % 2] - m_new).astype(BF16))

        def values(c, alpha):
            cols = slice(c * cw, (c + 1) * cw)
            acc_sc[:, cols] = alpha * acc_sc[:, cols] + jnp.dot(ct, p_buf[c % 2], preferred_element_type=F32)

        stats = {}
        for t in range(n_chunks + 2):
            if t < n_chunks:
                stats[t] = scores(t)
            if 0 <= t - 1 < n_chunks:
                probs(t - 1, stats[t - 1][0])
            if 0 <= t - 2 < n_chunks:
                values(t - 2, stats[t - 2][1])

    @pl.when(kj < last_kj)
    def _():
        process(False)

    @pl.when(kj == last_kj)
    def _():
        process(True)
        for h in range(MLA_HEADS):
            cols = slice(h * tq, (h + 1) * tq)
            o_t = acc_sc[0:KV_LORA, cols] * (1.0 / acc_sc[KV_LORA:KV_LORA + 1, cols])
            o_ref[0, :, h * KV_LORA:(h + 1) * KV_LORA] = jnp.transpose(o_t).astype(o_ref.dtype)


def _mla_prompt_call(qt, kv, ct, *, tq, tk):
    n_b, n_qb, _, n_cols = qt.shape
    seq = n_qb * tq
    assert tq & (tq - 1) == 0
    cw = min(n_cols, MLA_CHUNK_COLS)
    qi_l, kj_l = [], []
    for i in range(n_qb):
        for j in range((i * tq + tq - 1) // tk + 1):
            qi_l.append(i)
            kj_l.append(j)
    qi_t = jnp.asarray(np.array(qi_l, np.int32))
    kj_t = jnp.asarray(np.array(kj_l, np.int32))
    grid_spec = pltpu.PrefetchScalarGridSpec(
        num_scalar_prefetch=2,
        grid=(n_b, len(qi_l)),
        in_specs=[
            pl.BlockSpec((1, 1, QK_LAT, n_cols), lambda b, s, qi, kj: (b, qi[s], 0, 0)),
            pl.BlockSpec((1, tk, QK_LAT), lambda b, s, qi, kj: (b, kj[s], 0)),
            pl.BlockSpec((1, KV_LORA, tk), lambda b, s, qi, kj: (b, 0, kj[s])),
        ],
        out_specs=pl.BlockSpec((1, tq, LAT_WIDTH), lambda b, s, qi, kj: (b, qi[s], 0)),
        scratch_shapes=[pltpu.VMEM((1, n_cols), F32), pltpu.VMEM((KV_LORA + ONES_ROWS, n_cols), F32),
                        pltpu.VMEM((2, tk, cw), F32), pltpu.VMEM((2, tk, cw), BF16)],
    )
    return pl.pallas_call(
        functools.partial(_mla_prompt_kernel, tq=tq, tk=tk, cw=cw),
        grid_spec=grid_spec,
        out_shape=jax.ShapeDtypeStruct((n_b, seq, LAT_WIDTH), BF16),
        compiler_params=pltpu.CompilerParams(dimension_semantics=("arbitrary", "arbitrary"),
                                             vmem_limit_bytes=VMEM_LIMIT),
        name="mla_prompt",
    )(qi_t, kj_t, qt, kv, ct)


def _softmax_update(s, v, m_sc, l_sc, acc_sc):
    m_prev = m_sc[...]
    m_new = jnp.maximum(m_prev, jnp.max(s, axis=1, keepdims=True))
    alpha = jnp.exp2(m_prev - m_new)
    p = jnp.exp2(s - m_new)
    l_sc[...] = alpha * l_sc[...] + jnp.sum(p, axis=1, keepdims=True)
    acc_sc[...] = alpha * acc_sc[...] + jnp.dot(p.astype(BF16), v, preferred_element_type=F32)
    m_sc[...] = m_new


def _mla_decode_kernel(pt_ref, q_ref, kvn_ref, lat_hbm, krt_hbm, o_ref,
                       lat_buf, krt_buf, sems, m_sc, l_sc, acc_sc, *, n_pages, page, n_new, kc):
    b = pl.program_id(0)
    slot = b % 2
    rows = MLA_HEADS * n_new

    def page_copies(seq, slot_, i):
        pg = pt_ref[seq * n_pages + i]
        off = i * page if isinstance(i, int) else pl.multiple_of(i * page, page)
        return (pltpu.make_async_copy(lat_hbm.at[pg], lat_buf.at[slot_, pl.ds(off, page), :], sems.at[0, slot_]),
                pltpu.make_async_copy(krt_hbm.at[pg], krt_buf.at[slot_, i], sems.at[1, slot_]))

    def for_pages(seq, slot_, lo, hi, action):
        def body(i, carry):
            for cp in page_copies(seq, slot_, i):
                action(cp)
            return carry
        lax.fori_loop(lo, hi, body, 0, unroll=DMA_LOOP_UNROLL)

    start = lambda cp: cp.start()
    wait = lambda cp: cp.wait()
    last = pl.num_programs(0) - 1

    @pl.when(b == 0)
    def _():
        for_pages(0, 0, 0, n_pages, start)

    for_pages(b, slot, 0, n_pages, wait)
    nxt = jnp.minimum(b + 1, last)
    n_kc = n_pages * page // kc

    m_sc[...] = jnp.full(m_sc.shape, -jnp.inf, F32)
    l_sc[...] = jnp.zeros(l_sc.shape, F32)
    acc_sc[...] = jnp.zeros(acc_sc.shape, F32)
    q = q_ref[0].reshape(rows, QK_LAT).astype(BF16)
    q_lat = q[:, :KV_LORA]
    q_rope = q[:, KV_LORA:]
    contract_last = (((1,), (1,)), ((), ()))
    for c in range(n_kc):
        for i in range(c * n_pages // n_kc, (c + 1) * n_pages // n_kc):
            for cp in page_copies(nxt, 1 - slot, i):
                cp.start()
        lat = lat_buf[slot, c * kc:(c + 1) * kc, :].astype(BF16)
        krt = jnp.concatenate([krt_buf[slot, i] for i in range(c * kc // page, (c + 1) * kc // page)],
                              axis=1).astype(BF16)
        s = (lax.dot_general(q_lat, lat, contract_last, preferred_element_type=F32)
             + jnp.dot(q_rope, krt, preferred_element_type=F32))
        _softmax_update(s, lat, m_sc, l_sc, acc_sc)

    kn = jnp.concatenate([kvn_ref[0], jnp.zeros((LANES - n_new, QK_LAT), F32)], axis=0).astype(BF16)
    sn = lax.dot_general(q, kn, contract_last, preferred_element_type=F32)
    tpos = lax.broadcasted_iota(jnp.int32, (rows, LANES), 0) & (n_new - 1)
    jpos = lax.broadcasted_iota(jnp.int32, (rows, LANES), 1)
    sn = jnp.where(jpos <= tpos, sn, -jnp.inf)
    _softmax_update(sn, kn[:, :KV_LORA], m_sc, l_sc, acc_sc)
    inv_l = 1.0 / l_sc[...]
    for h in range(MLA_HEADS):
        r = slice(h * n_new, (h + 1) * n_new)
        o_ref[0, :, h * KV_LORA:(h + 1) * KV_LORA] = (acc_sc[r, :] * inv_l[r, :]).astype(o_ref.dtype)

    @pl.when(b == last)
    def _():
        for_pages(nxt, 1 - slot, 0, n_pages, wait)


def _mla_decode_call(q, kv_new, lat_pool, krt_pool, page_table):
    n_b, _, n_new, _ = q.shape
    assert n_new & (n_new - 1) == 0 and n_new <= LANES
    n_pages = page_table.shape[1]
    page = lat_pool.shape[1]
    past = n_pages * page
    kc = _pick_tile(past, 4096)
    grid_spec = pltpu.PrefetchScalarGridSpec(
        num_scalar_prefetch=1,
        grid=(n_b,),
        in_specs=[pl.BlockSpec((1, MLA_HEADS, n_new, QK_LAT), lambda b, pt: (b, 0, 0, 0)),
                  pl.BlockSpec((1, n_new, QK_LAT), lambda b, pt: (b, 0, 0)),
                  pl.BlockSpec(memory_space=pl.ANY),
                  pl.BlockSpec(memory_space=pl.ANY)],
        out_specs=pl.BlockSpec((1, n_new, LAT_WIDTH), lambda b, pt: (b, 0, 0)),
        scratch_shapes=[pltpu.VMEM((2, past, KV_LORA), F32), pltpu.VMEM((2, n_pages, QK_ROPE, page), F32),
                        pltpu.SemaphoreType.DMA((2, 2)),
                        pltpu.VMEM((MLA_HEADS * n_new, 1), F32), pltpu.VMEM((MLA_HEADS * n_new, 1), F32),
                        pltpu.VMEM((MLA_HEADS * n_new, KV_LORA), F32)],
    )
    return pl.pallas_call(
        functools.partial(_mla_decode_kernel, n_pages=n_pages, page=page, n_new=n_new, kc=kc),
        grid_spec=grid_spec,
        out_shape=jax.ShapeDtypeStruct((n_b, n_new, LAT_WIDTH), BF16),
        compiler_params=pltpu.CompilerParams(dimension_semantics=("arbitrary",),
                                             vmem_limit_bytes=VMEM_LIMIT),
        name="mla_decode",
    )(page_table.reshape(-1), q, kv_new, lat_pool, krt_pool)


def _gla_levels(chunk):
    levels = []
    h = chunk // 2
    while h >= 1:
        levels.append(h)
        h //= 2
    return levels


def _gla_decay_matrix(chunk):
    t = np.arange(chunk)[:, None]
    u = np.arange(chunk)[None, :]
    blocks = [(u <= t), (u > t)]
    for h in _gla_levels(chunk):
        off = t % (2 * h)
        mid = t - off + h
        upper = (off >= h) & (u >= mid) & (u <= t)
        lower = (off < h) & (u > t) & (u < mid)
        blocks.append(upper | lower)
    return np.concatenate(blocks, axis=0).astype(np.float32)


def _gla_chunk(qk, v, la, g_mat, gn, s_sc, o_ref, r0, *, chunk):
    levels = _gla_levels(chunk)
    la_hi = la.astype(BF16)
    la_lo = (la - la_hi.astype(F32)).astype(BF16)
    e_all = jnp.dot(g_mat, jnp.concatenate([la_hi, la_lo], axis=0),
                    preferred_element_type=F32)
    q = qk[:, :GLA_QK] * (GLA_DK ** -0.5)
    k = qk[:, GLA_QK:]
    e_cum = e_all[0:chunk]
    q_in = q * jnp.exp(e_cum)
    k_out = k * jnp.exp(e_all[chunk:2 * chunk])
    row = lax.broadcasted_iota(jnp.int32, (chunk, GLA_QK), 0)
    q_lv, k_lv = [q], [k]
    for li, h in enumerate(levels):
        w = jnp.exp(e_all[(2 + li) * chunk:(3 + li) * chunk])
        upper = (row & (2 * h - 1)) >= h
        q_lv.append(jnp.where(upper, q * w, 0.0))
        k_lv.append(jnp.where(upper, 0.0, k * w))
    contract_last = (((1,), (1,)), ((), ()))
    contract_first = (((0,), (0,)), ((), ()))
    head_of_lane = lax.broadcasted_iota(jnp.int32, (chunk, GLA_QK), 1) // GLA_DK
    stack_heads = lambda x: jnp.concatenate(
        [jnp.where(head_of_lane == hd, x, 0.0) for hd in range(GLA_HEADS)], axis=0).astype(BF16)
    trow = lax.broadcasted_iota(jnp.int32, (GLA_HEADS * chunk, chunk), 0) & (chunk - 1)
    tcol = lax.broadcasted_iota(jnp.int32, (GLA_HEADS * chunk, chunk), 1)
    masks = [trow == tcol] + [(trow // (2 * h)) == (tcol // (2 * h)) for h in levels]
    a_all = jnp.zeros((GLA_HEADS * chunk, chunk), F32)
    for ql, kl, mk in zip(q_lv, k_lv, masks):
        a_all = a_all + jnp.where(mk, lax.dot_general(stack_heads(ql), kl.astype(BF16), contract_last,
                                                      preferred_element_type=F32), 0.0)
    a_all = a_all.astype(BF16)
    lane = lax.broadcasted_iota(jnp.int32, (chunk, LANES), 1)
    decay_row = e_cum[chunk - 1:chunk, :]
    for p in range(GLA_HEADS // 2):
        ls = slice(p * LANES, (p + 1) * LANES)
        s_pair = s_sc[ls, :]
        decay = jnp.exp(jnp.transpose(jnp.broadcast_to(decay_row[:, ls], (LANES, LANES))))
        k_out_pair = k_out[:, ls].astype(BF16)
        new_rows = []
        for half in range(2):
            hd = 2 * p + half
            in_head = (lane // GLA_DK) == half
            v_h = v[:, hd * GLA_DV:(hd + 1) * GLA_DV]
            qi = jnp.where(in_head, q_in[:, ls], 0.0).astype(BF16)
            lhs = jnp.concatenate([qi, a_all[hd * chunk:(hd + 1) * chunk]], axis=1)
            rhs = jnp.concatenate([s_pair.astype(BF16), v_h], axis=0)
            o = jnp.dot(lhs, rhs, preferred_element_type=F32)
            o_ref[pl.ds(r0, chunk), hd * GLA_DV:(hd + 1) * GLA_DV] = (_rms(o) * gn).astype(o_ref.dtype)
            upd = lax.dot_general(k_out_pair, v_h, contract_first, preferred_element_type=F32)
            hr = slice(half * GLA_DK, (half + 1) * GLA_DK)
            new_rows.append(decay[hr, :] * s_pair[hr, :] + upd[hr, :])
        s_sc[ls, :] = jnp.concatenate(new_rows, axis=0)


def _gla_kernel(*refs, chunk, n_sub, has_init):
    if has_init:
        qk_ref, v_ref, la_ref, g_ref, gn_ref, s0_ref, o_ref, st_ref, s_sc = refs
    else:
        qk_ref, v_ref, la_ref, g_ref, gn_ref, o_ref, st_ref, s_sc = refs
    j = pl.program_id(1)

    @pl.when(j == 0)
    def _():
        s_sc[...] = s0_ref[0] if has_init else jnp.zeros(s_sc.shape, F32)

    g_mat = g_ref[...]
    gn = gn_ref[...]

    def body(c, carry):
        r0 = pl.multiple_of(c * chunk, chunk)
        _gla_chunk(qk_ref[pl.ds(r0, chunk), :], v_ref[pl.ds(r0, chunk), :], la_ref[pl.ds(r0, chunk), :],
                   g_mat, gn, s_sc, o_ref, r0, chunk=chunk)
        return carry

    lax.fori_loop(0, n_sub, body, 0, unroll=min(n_sub, GLA_UNROLL))

    @pl.when(j == pl.num_programs(1) - 1)
    def _():
        st_ref[0] = s_sc[...]


def _gla_call(qk, v, la, gn, s0, *, n_seq, seq_len, rows, chunk):
    per = seq_len // rows
    g_np = _gla_decay_matrix(chunk)
    g_mat = jnp.asarray(np.concatenate([g_np, g_np], axis=1), BF16)
    tok = lambda width: pl.BlockSpec((rows, width), lambda b, j: (b * per + j, 0))
    full = lambda a: pl.BlockSpec(a.shape, lambda b, j: (0,) * a.ndim)
    state = pl.BlockSpec((1, GLA_QK, GLA_DV), lambda b, j: (b, 0, 0))
    has_init = s0 is not None
    args = [qk, v, la, g_mat, gn] + ([s0] if has_init else [])
    in_specs = [tok(2 * GLA_QK), tok(GLA_WIDTH), tok(GLA_QK), full(g_mat), full(gn)] + ([state] if has_init else [])
    return pl.pallas_call(
        functools.partial(_gla_kernel, chunk=chunk, n_sub=rows // chunk, has_init=has_init),
        grid=(n_seq, per),
        in_specs=in_specs,
        out_specs=(tok(GLA_WIDTH), state),
        out_shape=(jax.ShapeDtypeStruct((n_seq * seq_len, GLA_WIDTH), BF16),
                   jax.ShapeDtypeStruct((n_seq, GLA_QK, GLA_DV), F32)),
        scratch_shapes=[pltpu.VMEM((GLA_QK, GLA_DV), F32)],
        compiler_params=pltpu.CompilerParams(dimension_semantics=("arbitrary", "arbitrary"),
                                             vmem_limit_bytes=VMEM_LIMIT),
        name="gla",
    )(*args)


def _out_kernel(x_ref, olat_ref, gm_ref, ogla_ref, gg_ref, p_ref, wuv_ref, wout_ref, wpg_ref, wpp_ref,
                gfin_ref, y_ref):
    o_mla = jnp.dot(olat_ref[...], wuv_ref[...], preferred_element_type=F32) * gm_ref[...].astype(F32)
    o_gla = ogla_ref[...].astype(F32) * gg_ref[...].astype(F32)
    mix = jnp.concatenate([o_mla, o_gla], axis=1).astype(BF16)
    h = x_ref[...] + jnp.dot(mix, wout_ref[...], preferred_element_type=F32)
    gate = jnp.dot(h.astype(BF16), wpg_ref[...], preferred_element_type=F32)
    gate = 1.0 / (1.0 + jnp.exp(-gate))
    emb = jnp.dot(p_ref[...].astype(BF16), wpp_ref[...], preferred_element_type=F32)
    h = h + gate * emb
    y_ref[...] = _rms(h) * gfin_ref[...]


def _out_call(x2d, o_lat, gm, o_gla, gg, p2d, w, *, ts):
    n_tok = x2d.shape[0]
    tok = lambda width: pl.BlockSpec((ts, width), lambda i: (i, 0))
    full = lambda a: pl.BlockSpec(a.shape, lambda i: (0,) * a.ndim)
    weights = (w["w_uv_bd"], w["w_out"], w["w_pg"], w["w_pp"], w["g_final"])
    return pl.pallas_call(
        _out_kernel,
        grid=(n_tok // ts,),
        in_specs=[tok(D_MODEL), tok(LAT_WIDTH), tok(MLA_WIDTH), tok(GLA_WIDTH), tok(GLA_WIDTH), tok(P_DIM)]
        + [full(a) for a in weights],
        out_specs=tok(D_MODEL),
        out_shape=jax.ShapeDtypeStruct((n_tok, D_MODEL), F32),
        compiler_params=pltpu.CompilerParams(dimension_semantics=("arbitrary",),
                                             vmem_limit_bytes=VMEM_LIMIT),
        name="out",
    )(x2d, o_lat, gm, o_gla, gg, p2d, *weights)


def _prep_weights(g_mix_norm, w_in, g_qnorm, w_qup, g_kvnorm, w_uk, w_uv, w_gla_a2, b_gla_a, g_gla_onorm,
                  w_out, w_ple_gate, w_ple_proj, g_final):
    o_kv = Q_LORA
    o_gm = o_kv + KV_LORA + QK_ROPE
    o_gq = o_gm + MLA_WIDTH
    o_gk = o_gq + GLA_QK
    o_gv = o_gk + GLA_QK
    o_a = o_gv + GLA_WIDTH
    o_gg = o_a + GLA_GATE_RANK
    pad = jnp.zeros((D_MODEL, 512 - (Q_LORA + KV_LORA + QK_ROPE + GLA_GATE_RANK)), F32)
    w_all = jnp.concatenate([
        w_in[:, 0:o_gm], w_in[:, o_a:o_gg], pad,
        w_in[:, o_gm:o_gq],
        w_in[:, o_gq:o_gv],
        w_in[:, o_gv:o_a],
        w_in[:, o_gg:],
    ], axis=1).astype(BF16)
    assert w_all.shape == (D_MODEL, Z_WIDTH)
    w_qup3 = w_qup.reshape(Q_LORA, MLA_HEADS, QK_NOPE + QK_ROPE)
    w_qup_p = jnp.concatenate([w_qup3[:, :, :QK_NOPE].reshape(Q_LORA, MLA_HEADS * QK_NOPE),
                               w_qup3[:, :, QK_NOPE:].reshape(Q_LORA, MLA_HEADS * QK_ROPE)], axis=1).astype(BF16)
    eye = jnp.eye(MLA_HEADS, dtype=F32)
    w_uk_bd = (eye[:, None, :, None] * jnp.transpose(w_uk, (1, 2, 0))[:, :, None, :]
               ).reshape(MLA_HEADS * QK_NOPE, LAT_WIDTH).astype(BF16)
    w_uv_bd = (eye[:, None, :, None] * jnp.transpose(w_uv, (1, 0, 2))[:, :, None, :]
               ).reshape(LAT_WIDTH, MLA_WIDTH).astype(BF16)
    w_a2 = jnp.zeros((LANES, GLA_QK), F32).at[QK_ROPE:QK_ROPE + GLA_GATE_RANK].set(w_gla_a2).astype(BF16)
    row = lambda a: a.reshape(1, -1).astype(F32)
    return dict(g_mix=row(g_mix_norm), w_all=w_all, g_qn=row(g_qnorm), w_qup=w_qup_p, w_uk_bd=w_uk_bd,
                g_kvn=row(g_kvnorm), w_a2=w_a2, b_a=row(b_gla_a), g_gla=row(g_gla_onorm),
                w_uv_bd=w_uv_bd, w_out=w_out.astype(BF16), w_pg=w_ple_gate.astype(BF16),
                w_pp=w_ple_proj.astype(BF16), g_final=row(g_final))


def _rope_tables(pos):
    inv = 1.0 / (ROPE_THETA ** (jnp.arange(0, QK_ROPE, 2, dtype=F32) / QK_ROPE))
    ang = pos[:, None] * inv[None, :]
    cos, sin = jnp.cos(ang), jnp.sin(ang)
    reps = LANES // QK_ROPE
    return (jnp.tile(jnp.concatenate([cos, cos], axis=1), (1, reps)),
            jnp.tile(jnp.concatenate([-sin, sin], axis=1), (1, reps)))


def _pick_tile(n, pref):
    t = min(n, pref)
    while n % t:
        t //= 2
    return t


def kernel(x_prompt, x_sample, p_prompt, p_sample, cache_ckv, cache_krope, state_gla, page_table, g_mix_norm, w_in, g_qnorm, w_qup, g_kvnorm, w_uk, w_uv, w_gla_a2, b_gla_a, g_gla_onorm, w_out, w_ple_gate, w_ple_proj, g_final):
    n_b, seq, _ = x_prompt.shape
    n_d, n_new, _ = x_sample.shape
    assert w_in.shape[0] == 1, "single-layer stack"
    page = cache_ckv.shape[2]
    past_len = page_table.shape[1] * page
    w = _prep_weights(g_mix_norm[0], w_in[0], g_qnorm[0], w_qup[0], g_kvnorm[0], w_uk[0], w_uv[0],
                      w_gla_a2[0], b_gla_a[0], g_gla_onorm[0], w_out[0], w_ple_gate[0], w_ple_proj[0], g_final)

    ts_p = _pick_tile(seq, 512)
    cos_p, sin_p = _rope_tables(jnp.arange(seq, dtype=F32))
    xp = x_prompt.reshape(n_b * seq, D_MODEL)
    tq = _pick_tile(seq, 512)
    tk = _pick_tile(seq, 512)
    (qt_p, kv_p, ckv_p, kr_p, gm_p, gqk_p, gv_p, la_p, gg_p, ct_p) = _proj_call(
        xp, cos_p, sin_p, w, n_seq=n_b, seq_len=seq, ts=ts_p, q_dtype=BF16, table_period=seq // ts_p, tq=tq)
    olat_p = _mla_prompt_call(qt_p, kv_p.reshape(n_b, seq, QK_LAT), ct_p, tq=tq, tk=tk)
    chunk_p = min(GLA_CHUNK, seq)
    ogla_p, st_p = _gla_call(gqk_p, gv_p, la_p, w["g_gla"], None, n_seq=n_b, seq_len=seq,
                             rows=_pick_tile(seq, 512), chunk=chunk_p)
    y_p = _out_call(xp, olat_p.reshape(n_b * seq, LAT_WIDTH), gm_p, ogla_p, gg_p,
                    p_prompt[0].reshape(n_b * seq, P_DIM), w, ts=ts_p)

    n_tok_s = n_d * n_new
    ts_s = _pick_tile(n_tok_s, 512)
    cos_s, sin_s = _rope_tables(past_len + jnp.arange(n_new, dtype=F32))
    cos_s = jnp.tile(cos_s, (ts_s // n_new, 1))
    sin_s = jnp.tile(sin_s, (ts_s // n_new, 1))
    xs = x_sample.reshape(n_tok_s, D_MODEL)
    (q_s, kv_s, ckv_s, kr_s, gm_s, gqk_s, gv_s, la_s, gg_s) = _proj_call(
        xs, cos_s, sin_s, w, n_seq=n_d, seq_len=n_new, ts=ts_s, q_dtype=F32, table_period=1)
    olat_s = _mla_decode_call(q_s, kv_s.reshape(n_d, n_new, QK_LAT), cache_ckv[0],
                              jnp.swapaxes(cache_krope[0], 1, 2), page_table)
    chunk_s = max(n_new, GLA_SAMPLE_CHUNK)
    pad_tok = lambda a: jnp.pad(a.reshape(n_d, n_new, -1), ((0, 0), (0, chunk_s - n_new), (0, 0))
                                ).reshape(n_d * chunk_s, -1)
    ogla_s, st_s = _gla_call(pad_tok(gqk_s), pad_tok(gv_s), pad_tok(la_s), w["g_gla"],
                             state_gla[0].reshape(n_d, GLA_QK, GLA_DV),
                             n_seq=n_d, seq_len=chunk_s, rows=chunk_s, chunk=chunk_s)
    ogla_s = ogla_s.reshape(n_d, chunk_s, GLA_WIDTH)[:, :n_new].reshape(n_tok_s, GLA_WIDTH)
    y_s = _out_call(xs, olat_s.reshape(n_tok_s, LAT_WIDTH), gm_s, ogla_s, gg_s,
                    p_sample[0].reshape(n_tok_s, P_DIM), w, ts=ts_s)

    return (y_p.reshape(n_b, seq, D_MODEL),
            y_s.reshape(n_d, n_new, D_MODEL),
            ckv_p.reshape(1, n_b, seq, KV_LORA),
            kr_p.reshape(1, n_b, seq, QK_ROPE),
            st_p.reshape(1, n_b, GLA_HEADS, GLA_DK, GLA_DV),
            ckv_s.reshape(1, n_d, n_new, KV_LORA),
            kr_s.reshape(1, n_d, n_new, QK_ROPE),
            st_s.reshape(1, n_d, GLA_HEADS, GLA_DK, GLA_DV))
```

```python
import functools

import numpy as np
import jax
import jax.numpy as jnp
from jax import lax
from jax.experimental import pallas as pl
from jax.experimental.pallas import tpu as pltpu

F32 = jnp.float32
BF16 = jnp.bfloat16

D_MODEL = 1024
MLA_HEADS = 8
QK_NOPE = 64
QK_ROPE = 32
V_HEAD = 64
KV_LORA = 128
Q_LORA = 256
MLA_WIDTH = MLA_HEADS * V_HEAD
GLA_HEADS = 4
GLA_DK = 64
GLA_DV = 128
GLA_WIDTH = GLA_HEADS * GLA_DV
GLA_QK = GLA_HEADS * GLA_DK
GLA_GATE_RANK = 16
GLA_TAU = 16.0
P_DIM = 256
NORM_EPS = 1e-6
ROPE_THETA = 10000.0
QK_LAT = KV_LORA + QK_ROPE
LAT_WIDTH = MLA_HEADS * KV_LORA
Z_WIDTH = 2560
GLA_CHUNK = 64
LOG2_E = 1.4426950408889634
GLA_SAMPLE_CHUNK = 16
GLA_UNROLL = 4
DECODE_CHUNK_KEYS = 4096
MLA_TILES_PER_STEP = 2
MLA_CHUNK_COLS = 1024
ONES_ROWS = 16
DMA_LOOP_UNROLL = 8

LANES = 128
VMEM_LIMIT = 56 * 1024 * 1024


def _rms(x):
    return x * lax.rsqrt(jnp.mean(x * x, axis=-1, keepdims=True) + NORM_EPS)


def _silu(x):
    return x * (1.0 / (1.0 + jnp.exp(-x)))


def _swap16(x):
    n = x.shape[-1]
    lane = lax.broadcasted_iota(jnp.int32, x.shape, x.ndim - 1)
    fwd = pltpu.roll(x, n - 16, axis=x.ndim - 1)
    bwd = pltpu.roll(x, 16, axis=x.ndim - 1)
    return jnp.where((lane & 31) < 16, fwd, bwd)


def _proj_kernel(x_ref, cos_ref, sin_ref, gmix_ref, wall_ref, gqn_ref, wqup_ref, wukbd_ref,
                 gkvn_ref, wa2_ref, ba_ref,
                 q_out, kv_out, ckv_out, kr_out, gm_out, gqk_out, gv_out, la_out, gg_out, ct_out=None,
                 *, scale, tq):
    x = x_ref[...]
    xn = (_rms(x) * gmix_ref[...]).astype(BF16)
    z = jnp.dot(xn, wall_ref[...], preferred_element_type=F32)
    cos = cos_ref[...]
    sin = sin_ref[...]

    qn = (_rms(z[:, 0:Q_LORA]) * gqn_ref[...]).astype(BF16)
    q = jnp.dot(qn, wqup_ref[...], preferred_element_type=F32)
    nope_w = MLA_HEADS * QK_NOPE
    q_lat = jnp.dot(q[:, :nope_w].astype(BF16), wukbd_ref[...],
                    preferred_element_type=F32) * scale
    qr = q[:, nope_w:]
    cos2 = jnp.concatenate([cos, cos], axis=1)
    sin2 = jnp.concatenate([sin, sin], axis=1)
    qr = (qr * cos2 + _swap16(qr) * sin2) * scale
    if tq is None:
        nb, _, lb, _ = q_out.shape
        for h in range(MLA_HEADS):
            q_h = jnp.concatenate([q_lat[:, h * KV_LORA:(h + 1) * KV_LORA],
                                   qr[:, h * QK_ROPE:(h + 1) * QK_ROPE]], axis=1)
            q_out[:, h, :, :] = q_h.reshape(nb, lb, QK_LAT).astype(q_out.dtype)
    else:
        q_lat_t = jnp.transpose(q_lat).astype(q_out.dtype)
        qr_t = jnp.transpose(qr).astype(q_out.dtype)
        for h in range(MLA_HEADS):
            for j in range(q_out.shape[1]):
                tok = slice(j * tq, (j + 1) * tq)
                q_out[0, j, 0:KV_LORA, h * tq:(h + 1) * tq] = q_lat_t[h * KV_LORA:(h + 1) * KV_LORA, tok]
                q_out[0, j, KV_LORA:QK_LAT, h * tq:(h + 1) * tq] = qr_t[h * QK_ROPE:(h + 1) * QK_ROPE, tok]

    ckv = _rms(z[:, 256:384]) * gkvn_ref[...]
    ckv_out[...] = ckv
    blk = z[:, 384:512]
    kr = (blk * cos + _swap16(blk) * sin)[:, :QK_ROPE]
    kr_out[...] = kr
    kv_out[:, 0:KV_LORA] = ckv.astype(kv_out.dtype)
    kv_out[:, KV_LORA:QK_LAT] = kr.astype(kv_out.dtype)
    if tq is not None:
        ct_out[0] = jnp.transpose(ckv).astype(ct_out.dtype)

    a_pre = jnp.dot(blk.astype(BF16), wa2_ref[...], preferred_element_type=F32) + ba_ref[...]
    la_out[...] = (jnp.minimum(a_pre, 0.0) - jnp.log1p(jnp.exp(-jnp.abs(a_pre)))) * (1.0 / GLA_TAU)

    gm_out[...] = _silu(z[:, 512:1024]).astype(gm_out.dtype)
    gqk_out[...] = z[:, 1024:1536]
    gv_out[...] = z[:, 1536:2048].astype(gv_out.dtype)
    gg_out[...] = _silu(z[:, 2048:2560]).astype(gg_out.dtype)


def _proj_call(x2d, cos_t, sin_t, w, *, n_seq, seq_len, ts, q_dtype, table_period, tq=None):
    n_tok = x2d.shape[0]
    n_tiles = n_tok // ts
    tok = lambda width: pl.BlockSpec((ts, width), lambda i: (i, 0))
    full = lambda a: pl.BlockSpec(a.shape, lambda i: (0,) * a.ndim)
    tab = pl.BlockSpec((ts, LANES), lambda i: (i % table_period, 0))
    if tq is None:
        assert ts % seq_len == 0
        q_shape = jax.ShapeDtypeStruct((n_seq, MLA_HEADS, seq_len, QK_LAT), q_dtype)
        q_spec = pl.BlockSpec((ts // seq_len, MLA_HEADS, seq_len, QK_LAT), lambda i: (i, 0, 0, 0))
        extra_shape, extra_spec = (), ()
    else:
        assert seq_len % ts == 0 and ts % tq == 0
        per = seq_len // ts
        q_shape = jax.ShapeDtypeStruct((n_seq, seq_len // tq, QK_LAT, MLA_HEADS * tq), q_dtype)
        q_spec = pl.BlockSpec((1, ts // tq, QK_LAT, MLA_HEADS * tq), lambda i: (i // per, i % per, 0, 0))
        extra_shape = (jax.ShapeDtypeStruct((n_seq, KV_LORA, seq_len), q_dtype),)
        extra_spec = (pl.BlockSpec((1, KV_LORA, ts), lambda i: (i // per, 0, i % per)),)
    out_shape = (
        q_shape,
        jax.ShapeDtypeStruct((n_tok, QK_LAT), q_dtype),
        jax.ShapeDtypeStruct((n_tok, KV_LORA), F32),
        jax.ShapeDtypeStruct((n_tok, QK_ROPE), F32),
        jax.ShapeDtypeStruct((n_tok, MLA_WIDTH), BF16),
        jax.ShapeDtypeStruct((n_tok, 2 * GLA_QK), F32),
        jax.ShapeDtypeStruct((n_tok, GLA_WIDTH), BF16),
        jax.ShapeDtypeStruct((n_tok, GLA_QK), F32),
        jax.ShapeDtypeStruct((n_tok, GLA_WIDTH), BF16),
    ) + extra_shape
    out_specs = (
        q_spec,
        tok(QK_LAT), tok(KV_LORA), tok(QK_ROPE), tok(MLA_WIDTH), tok(2 * GLA_QK),
        tok(GLA_WIDTH), tok(GLA_QK), tok(GLA_WIDTH),
    ) + extra_spec
    weights = (w["g_mix"], w["w_all"], w["g_qn"], w["w_qup"], w["w_uk_bd"], w["g_kvn"],
               w["w_a2"], w["b_a"])
    return pl.pallas_call(
        functools.partial(_proj_kernel, scale=(QK_NOPE + QK_ROPE) ** -0.5 * LOG2_E, tq=tq),
        grid=(n_tiles,),
        in_specs=[tok(D_MODEL), tab, tab] + [full(a) for a in weights],
        out_specs=out_specs,
        out_shape=out_shape,
        compiler_params=pltpu.CompilerParams(dimension_semantics=("arbitrary",),
                                             vmem_limit_bytes=VMEM_LIMIT),
        name="proj",
    )(x2d, cos_t, sin_t, *weights)


def _mla_prompt_kernel(qi_ref, kj_ref, qt_ref, kv_ref, ct_ref, wuvt_ref, o_ref, m_sc, acc_sc, s_buf, p_buf,
                       *, tq, tk, cw, n_sub):
    step = pl.program_id(1)
    qi = qi_ref[step]
    kg = kj_ref[step]
    last_kj = (qi * tq + (tq - 1)) // tk
    n_cols = MLA_HEADS * tq

    @pl.when(kg == 0)
    def _():
        m_sc[...] = jnp.full(m_sc.shape, -jnp.inf, F32)
        acc_sc[...] = jnp.zeros(acc_sc.shape, F32)

    def process(sub, kj, masked):
        kv = kv_ref[0, sub * tk:(sub + 1) * tk, :]
        ct = jnp.concatenate([ct_ref[0, :, sub * tk:(sub + 1) * tk], jnp.ones((ONES_ROWS, tk), BF16)], axis=0)
        n_chunks = n_cols // cw

        def scores(c):
            cols = slice(c * cw, (c + 1) * cw)
            s = jnp.dot(kv, qt_ref[0, 0, :, cols], preferred_element_type=F32)
            if masked:
                kpos = kj * tk + lax.broadcasted_iota(jnp.int32, (tk, cw), 0)
                qpos = qi * tq + ((c * cw + lax.broadcasted_iota(jnp.int32, (tk, cw), 1)) & (tq - 1))
                s = jnp.where(kpos <= qpos, s, -jnp.inf)
            s_buf[c % 2] = s
            m_prev = m_sc[:, cols]
            m_new = jnp.maximum(m_prev, jnp.max(s, axis=0, keepdims=True))
            m_sc[:, cols] = m_new
            return m_new, jnp.exp2(m_prev - m_new)

        def probs(c, m_new):
            p_buf[c % 2] = jnp.exp2((s_buf[c % 2] - m_new).astype(BF16))

        def values(c, alpha):
            cols = slice(c * cw, (c + 1) * cw)
            acc_sc[:, cols] = alpha * acc_sc[:, cols] + jnp.dot(ct, p_buf[c % 2], preferred_element_type=F32)

        stats = {}
        for t in range(n_chunks + 2):
            if t < n_chunks:
                stats[t] = scores(t)
            if 0 <= t - 1 < n_chunks:
                probs(t - 1, stats[t - 1][0])
            if 0 <= t - 2 < n_chunks:
                values(t - 2, stats[t - 2][1])

    for sub in range(n_sub):
        kj = kg * n_sub + sub

        @pl.when(kj < last_kj)
        def _():
            process(sub, kj, False)

        @pl.when(kj == last_kj)
        def _():
            process(sub, kj, True)

    @pl.when(kg == last_kj // n_sub)
    def _():
        for pair in range(MLA_HEADS // 2):
            halves = []
            for h in (2 * pair, 2 * pair + 1):
                cols = slice(h * tq, (h + 1) * tq)
                o_lat_t = acc_sc[0:KV_LORA, cols] * (1.0 / acc_sc[KV_LORA:KV_LORA + 1, cols])
                halves.append(jnp.dot(wuvt_ref[h], o_lat_t.astype(BF16), preferred_element_type=F32))
            o_ref[0, :, pair * LANES:(pair + 1) * LANES] = jnp.transpose(
                jnp.concatenate(halves, axis=0)).astype(o_ref.dtype)


def _mla_prompt_call(qt, kv, ct, w_uv_t, *, tq, tk):
    n_b, n_qb, _, n_cols = qt.shape
    seq = n_qb * tq
    assert tq & (tq - 1) == 0
    cw = min(n_cols, MLA_CHUNK_COLS)
    n_sub = MLA_TILES_PER_STEP if seq % (MLA_TILES_PER_STEP * tk) == 0 else 1
    qi_l, kj_l = [], []
    for i in range(n_qb):
        for g in range(((i * tq + tq - 1) // tk) // n_sub + 1):
            qi_l.append(i)
            kj_l.append(g)
    qi_t = jnp.asarray(np.array(qi_l, np.int32))
    kj_t = jnp.asarray(np.array(kj_l, np.int32))
    grid_spec = pltpu.PrefetchScalarGridSpec(
        num_scalar_prefetch=2,
        grid=(n_b, len(qi_l)),
        in_specs=[
            pl.BlockSpec((1, 1, QK_LAT, n_cols), lambda b, s, qi, kj: (b, qi[s], 0, 0)),
            pl.BlockSpec((1, n_sub * tk, QK_LAT), lambda b, s, qi, kj: (b, kj[s], 0)),
            pl.BlockSpec((1, KV_LORA, n_sub * tk), lambda b, s, qi, kj: (b, 0, kj[s])),
            pl.BlockSpec(w_uv_t.shape, lambda b, s, qi, kj: (0, 0, 0)),
        ],
        out_specs=pl.BlockSpec((1, tq, MLA_WIDTH), lambda b, s, qi, kj: (b, qi[s], 0)),
        scratch_shapes=[pltpu.VMEM((1, n_cols), F32), pltpu.VMEM((KV_LORA + ONES_ROWS, n_cols), F32),
                        pltpu.VMEM((2, tk, cw), F32), pltpu.VMEM((2, tk, cw), BF16)],
    )
    return pl.pallas_call(
        functools.partial(_mla_prompt_kernel, tq=tq, tk=tk, cw=cw, n_sub=n_sub),
        grid_spec=grid_spec,
        out_shape=jax.ShapeDtypeStruct((n_b, seq, MLA_WIDTH), BF16),
        compiler_params=pltpu.CompilerParams(dimension_semantics=("arbitrary", "arbitrary"),
                                             vmem_limit_bytes=VMEM_LIMIT),
        name="mla_prompt",
    )(qi_t, kj_t, qt, kv, ct, w_uv_t)


def _mla_decode_kernel(pt_ref, q_ref, kvn_ref, lat_hbm, krt_hbm, o_ref,
                       lat_buf, krt_buf, sems, *, n_pages, page, n_new, kc):
    b = pl.program_id(0)
    slot = b % 2
    rows = MLA_HEADS * n_new

    def page_copies(seq, slot_, i):
        pg = pt_ref[seq * n_pages + i]
        off = i * page if isinstance(i, int) else pl.multiple_of(i * page, page)
        return (pltpu.make_async_copy(lat_hbm.at[pg], lat_buf.at[slot_, pl.ds(off, page), :], sems.at[0, slot_]),
                pltpu.make_async_copy(krt_hbm.at[pg], krt_buf.at[slot_, i], sems.at[1, slot_]))

    def for_pages(seq, slot_, lo, hi, action):
        def body(i, carry):
            for cp in page_copies(seq, slot_, i):
                action(cp)
            return carry
        lax.fori_loop(lo, hi, body, 0, unroll=DMA_LOOP_UNROLL)

    start = lambda cp: cp.start()
    wait = lambda cp: cp.wait()
    last = pl.num_programs(0) - 1

    @pl.when(b == 0)
    def _():
        for_pages(0, 0, 0, n_pages, start)

    for_pages(b, slot, 0, n_pages, wait)
    nxt = jnp.minimum(b + 1, last)
    n_kc = n_pages * page // kc

    q = q_ref[0].reshape(rows, QK_LAT).astype(BF16)
    q_lat = q[:, :KV_LORA]
    q_rope = q[:, KV_LORA:]
    contract_last = (((1,), (1,)), ((), ()))
    lats, scores = [], []
    for c in range(n_kc):
        for i in range(c * n_pages // n_kc, (c + 1) * n_pages // n_kc):
            for cp in page_copies(nxt, 1 - slot, i):
                cp.start()
        lat = lat_buf[slot, c * kc:(c + 1) * kc, :].astype(BF16)
        krt = jnp.concatenate([krt_buf[slot, i] for i in range(c * kc // page, (c + 1) * kc // page)],
                              axis=1).astype(BF16)
        lats.append(lat)
        scores.append(lax.dot_general(q_lat, lat, contract_last, preferred_element_type=F32)
                      + jnp.dot(q_rope, krt, preferred_element_type=F32))
    kn = jnp.concatenate([kvn_ref[0], jnp.zeros((LANES - n_new, QK_LAT), F32)], axis=0).astype(BF16)
    sn = lax.dot_general(q, kn, contract_last, preferred_element_type=F32)
    tpos = lax.broadcasted_iota(jnp.int32, (rows, LANES), 0) & (n_new - 1)
    jpos = lax.broadcasted_iota(jnp.int32, (rows, LANES), 1)
    lats.append(kn[:, :KV_LORA])
    scores.append(jnp.where(jpos <= tpos, sn, -jnp.inf))

    m = functools.reduce(jnp.maximum, [jnp.max(s, axis=1, keepdims=True) for s in scores])
    probs = [jnp.exp2(s - m) for s in scores]
    l = functools.reduce(jnp.add, [jnp.sum(p, axis=1, keepdims=True) for p in probs])
    acc = functools.reduce(jnp.add, [jnp.dot(p.astype(BF16), v, preferred_element_type=F32)
                                     for p, v in zip(probs, lats)])
    o = acc * (1.0 / l)
    for h in range(MLA_HEADS):
        o_ref[0, :, h * KV_LORA:(h + 1) * KV_LORA] = o[h * n_new:(h + 1) * n_new, :].astype(o_ref.dtype)

    @pl.when(b == last)
    def _():
        for_pages(nxt, 1 - slot, 0, n_pages, wait)


def _mla_decode_call(q, kv_new, lat_pool, krt_pool, page_table):
    n_b, _, n_new, _ = q.shape
    assert n_new & (n_new - 1) == 0 and n_new <= LANES
    n_pages = page_table.shape[1]
    page = lat_pool.shape[1]
    past = n_pages * page
    kc = _pick_tile(past, DECODE_CHUNK_KEYS)
    grid_spec = pltpu.PrefetchScalarGridSpec(
        num_scalar_prefetch=1,
        grid=(n_b,),
        in_specs=[pl.BlockSpec((1, MLA_HEADS, n_new, QK_LAT), lambda b, pt: (b, 0, 0, 0)),
                  pl.BlockSpec((1, n_new, QK_LAT), lambda b, pt: (b, 0, 0)),
                  pl.BlockSpec(memory_space=pl.ANY),
                  pl.BlockSpec(memory_space=pl.ANY)],
        out_specs=pl.BlockSpec((1, n_new, LAT_WIDTH), lambda b, pt: (b, 0, 0)),
        scratch_shapes=[pltpu.VMEM((2, past, KV_LORA), F32), pltpu.VMEM((2, n_pages, QK_ROPE, page), F32),
                        pltpu.SemaphoreType.DMA((2, 2))],
    )
    return pl.pallas_call(
        functools.partial(_mla_decode_kernel, n_pages=n_pages, page=page, n_new=n_new, kc=kc),
        grid_spec=grid_spec,
        out_shape=jax.ShapeDtypeStruct((n_b, n_new, LAT_WIDTH), BF16),
        compiler_params=pltpu.CompilerParams(dimension_semantics=("arbitrary",),
                                             vmem_limit_bytes=VMEM_LIMIT),
        name="mla_decode",
    )(page_table.reshape(-1), q, kv_new, lat_pool, krt_pool)


def _gla_levels(chunk):
    levels = []
    h = chunk // 2
    while h >= 1:
        levels.append(h)
        h //= 2
    return levels


def _gla_decay_matrix(chunk):
    t = np.arange(chunk)[:, None]
    u = np.arange(chunk)[None, :]
    blocks = [(u <= t), (u > t)]
    for h in _gla_levels(chunk):
        off = t % (2 * h)
        mid = t - off + h
        upper = (off >= h) & (u >= mid) & (u <= t)
        lower = (off < h) & (u > t) & (u < mid)
        blocks.append(upper | lower)
    return np.concatenate(blocks, axis=0).astype(np.float32)


def _gla_chunk(qk, v, la, g_mat, gn, s_sc, o_ref, r0, *, chunk):
    levels = _gla_levels(chunk)
    la_hi = la.astype(BF16)
    la_lo = (la - la_hi.astype(F32)).astype(BF16)
    e_all = jnp.dot(g_mat, jnp.concatenate([la_hi, la_lo], axis=0),
                    preferred_element_type=F32)
    q = qk[:, :GLA_QK] * (GLA_DK ** -0.5)
    k = qk[:, GLA_QK:]
    e_cum = e_all[0:chunk]
    q_in = q * jnp.exp(e_cum)
    k_out = k * jnp.exp(e_all[chunk:2 * chunk])
    row = lax.broadcasted_iota(jnp.int32, (chunk, GLA_QK), 0)
    q_lv, k_lv = [q], [k]
    for li, h in enumerate(levels):
        w = jnp.exp(e_all[(2 + li) * chunk:(3 + li) * chunk])
        upper = (row & (2 * h - 1)) >= h
        q_lv.append(jnp.where(upper, q * w, 0.0))
        k_lv.append(jnp.where(upper, 0.0, k * w))
    contract_last = (((1,), (1,)), ((), ()))
    contract_first = (((0,), (0,)), ((), ()))
    head_of_lane = lax.broadcasted_iota(jnp.int32, (chunk, GLA_QK), 1) // GLA_DK
    stack_heads = lambda x: jnp.concatenate(
        [jnp.where(head_of_lane == hd, x, 0.0) for hd in range(GLA_HEADS)], axis=0).astype(BF16)
    trow = lax.broadcasted_iota(jnp.int32, (GLA_HEADS * chunk, chunk), 0) & (chunk - 1)
    tcol = lax.broadcasted_iota(jnp.int32, (GLA_HEADS * chunk, chunk), 1)
    masks = [trow == tcol] + [(trow // (2 * h)) == (tcol // (2 * h)) for h in levels]
    a_all = jnp.zeros((GLA_HEADS * chunk, chunk), F32)
    for ql, kl, mk in zip(q_lv, k_lv, masks):
        a_all = a_all + jnp.where(mk, lax.dot_general(stack_heads(ql), kl.astype(BF16), contract_last,
                                                      preferred_element_type=F32), 0.0)
    a_all = a_all.astype(BF16)
    lane = lax.broadcasted_iota(jnp.int32, (chunk, LANES), 1)
    decay_row = e_cum[chunk - 1:chunk, :]
    for p in range(GLA_HEADS // 2):
        ls = slice(p * LANES, (p + 1) * LANES)
        s_pair = s_sc[ls, :]
        decay = jnp.exp(jnp.transpose(jnp.broadcast_to(decay_row[:, ls], (LANES, LANES))))
        k_out_pair = k_out[:, ls].astype(BF16)
        new_rows = []
        for half in range(2):
            hd = 2 * p + half
            in_head = (lane // GLA_DK) == half
            v_h = v[:, hd * GLA_DV:(hd + 1) * GLA_DV]
            qi = jnp.where(in_head, q_in[:, ls], 0.0).astype(BF16)
            lhs = jnp.concatenate([qi, a_all[hd * chunk:(hd + 1) * chunk]], axis=1)
            rhs = jnp.concatenate([s_pair.astype(BF16), v_h], axis=0)
            o = jnp.dot(lhs, rhs, preferred_element_type=F32)
            o_ref[pl.ds(r0, chunk), hd * GLA_DV:(hd + 1) * GLA_DV] = (_rms(o) * gn).astype(o_ref.dtype)
            upd = lax.dot_general(k_out_pair, v_h, contract_first, preferred_element_type=F32)
            hr = slice(half * GLA_DK, (half + 1) * GLA_DK)
            new_rows.append(decay[hr, :] * s_pair[hr, :] + upd[hr, :])
        s_sc[ls, :] = jnp.concatenate(new_rows, axis=0)


def _gla_kernel(*refs, chunk, n_sub, has_init):
    if has_init:
        qk_ref, v_ref, la_ref, g_ref, gn_ref, s0_ref, o_ref, st_ref, s_sc = refs
    else:
        qk_ref, v_ref, la_ref, g_ref, gn_ref, o_ref, st_ref, s_sc = refs
    j = pl.program_id(1)

    @pl.when(j == 0)
    def _():
        s_sc[...] = s0_ref[0] if has_init else jnp.zeros(s_sc.shape, F32)

    g_mat = g_ref[...]
    gn = gn_ref[...]

    def body(c, carry):
        r0 = pl.multiple_of(c * chunk, chunk)
        _gla_chunk(qk_ref[pl.ds(r0, chunk), :], v_ref[pl.ds(r0, chunk), :], la_ref[pl.ds(r0, chunk), :],
                   g_mat, gn, s_sc, o_ref, r0, chunk=chunk)
        return carry

    lax.fori_loop(0, n_sub, body, 0, unroll=min(n_sub, GLA_UNROLL))

    @pl.when(j == pl.num_programs(1) - 1)
    def _():
        st_ref[0] = s_sc[...]


def _gla_call(qk, v, la, gn, s0, *, n_seq, seq_len, rows, chunk):
    per = seq_len // rows
    g_np = _gla_decay_matrix(chunk)
    g_mat = jnp.asarray(np.concatenate([g_np, g_np], axis=1), BF16)
    tok = lambda width: pl.BlockSpec((rows, width), lambda b, j: (b * per + j, 0))
    full = lambda a: pl.BlockSpec(a.shape, lambda b, j: (0,) * a.ndim)
    state = pl.BlockSpec((1, GLA_QK, GLA_DV), lambda b, j: (b, 0, 0))
    has_init = s0 is not None
    args = [qk, v, la, g_mat, gn] + ([s0] if has_init else [])
    in_specs = [tok(2 * GLA_QK), tok(GLA_WIDTH), tok(GLA_QK), full(g_mat), full(gn)] + ([state] if has_init else [])
    return pl.pallas_call(
        functools.partial(_gla_kernel, chunk=chunk, n_sub=rows // chunk, has_init=has_init),
        grid=(n_seq, per),
        in_specs=in_specs,
        out_specs=(tok(GLA_WIDTH), state),
        out_shape=(jax.ShapeDtypeStruct((n_seq * seq_len, GLA_WIDTH), BF16),
                   jax.ShapeDtypeStruct((n_seq, GLA_QK, GLA_DV), F32)),
        scratch_shapes=[pltpu.VMEM((GLA_QK, GLA_DV), F32)],
        compiler_params=pltpu.CompilerParams(dimension_semantics=("arbitrary", "arbitrary"),
                                             vmem_limit_bytes=VMEM_LIMIT),
        name="gla",
    )(*args)


def _out_kernel(x_ref, omla_ref, gm_ref, ogla_ref, gg_ref, p_ref, wuv_ref, wout_ref, wpg_ref, wpp_ref,
                gfin_ref, y_ref, *, latent_in):
    if latent_in:
        o_mla = jnp.dot(omla_ref[...], wuv_ref[...], preferred_element_type=F32)
    else:
        o_mla = omla_ref[...].astype(F32)
    o_mla = o_mla * gm_ref[...].astype(F32)
    o_gla = ogla_ref[...].astype(F32) * gg_ref[...].astype(F32)
    mix = jnp.concatenate([o_mla, o_gla], axis=1).astype(BF16)
    h = x_ref[...] + jnp.dot(mix, wout_ref[...], preferred_element_type=F32)
    gate = jnp.dot(h.astype(BF16), wpg_ref[...], preferred_element_type=F32)
    gate = 1.0 / (1.0 + jnp.exp(-gate))
    emb = jnp.dot(p_ref[...].astype(BF16), wpp_ref[...], preferred_element_type=F32)
    h = h + gate * emb
    y_ref[...] = _rms(h) * gfin_ref[...]


def _out_call(x2d, o_mla, gm, o_gla, gg, p2d, w, *, ts):
    n_tok = x2d.shape[0]
    mla_w = o_mla.shape[1]
    tok = lambda width: pl.BlockSpec((ts, width), lambda i: (i, 0))
    full = lambda a: pl.BlockSpec(a.shape, lambda i: (0,) * a.ndim)
    weights = (w["w_uv_bd"], w["w_out"], w["w_pg"], w["w_pp"], w["g_final"])
    return pl.pallas_call(
        functools.partial(_out_kernel, latent_in=(mla_w == LAT_WIDTH)),
        grid=(n_tok // ts,),
        in_specs=[tok(D_MODEL), tok(mla_w), tok(MLA_WIDTH), tok(GLA_WIDTH), tok(GLA_WIDTH), tok(P_DIM)]
        + [full(a) for a in weights],
        out_specs=tok(D_MODEL),
        out_shape=jax.ShapeDtypeStruct((n_tok, D_MODEL), F32),
        compiler_params=pltpu.CompilerParams(dimension_semantics=("arbitrary",),
                                             vmem_limit_bytes=VMEM_LIMIT),
        name="out",
    )(x2d, o_mla, gm, o_gla, gg, p2d, *weights)


def _prep_weights(g_mix_norm, w_in, g_qnorm, w_qup, g_kvnorm, w_uk, w_uv, w_gla_a2, b_gla_a, g_gla_onorm,
                  w_out, w_ple_gate, w_ple_proj, g_final):
    o_kv = Q_LORA
    o_gm = o_kv + KV_LORA + QK_ROPE
    o_gq = o_gm + MLA_WIDTH
    o_gk = o_gq + GLA_QK
    o_gv = o_gk + GLA_QK
    o_a = o_gv + GLA_WIDTH
    o_gg = o_a + GLA_GATE_RANK
    pad = jnp.zeros((D_MODEL, 512 - (Q_LORA + KV_LORA + QK_ROPE + GLA_GATE_RANK)), F32)
    w_all = jnp.concatenate([
        w_in[:, 0:o_gm], w_in[:, o_a:o_gg], pad,
        w_in[:, o_gm:o_gq],
        w_in[:, o_gq:o_gv],
        w_in[:, o_gv:o_a],
        w_in[:, o_gg:],
    ], axis=1).astype(BF16)
    assert w_all.shape == (D_MODEL, Z_WIDTH)
    w_qup3 = w_qup.reshape(Q_LORA, MLA_HEADS, QK_NOPE + QK_ROPE)
    w_qup_p = jnp.concatenate([w_qup3[:, :, :QK_NOPE].reshape(Q_LORA, MLA_HEADS * QK_NOPE),
                               w_qup3[:, :, QK_NOPE:].reshape(Q_LORA, MLA_HEADS * QK_ROPE)], axis=1).astype(BF16)
    eye = jnp.eye(MLA_HEADS, dtype=F32)
    w_uk_bd = (eye[:, None, :, None] * jnp.transpose(w_uk, (1, 2, 0))[:, :, None, :]
               ).reshape(MLA_HEADS * QK_NOPE, LAT_WIDTH).astype(BF16)
    w_uv_bd = (eye[:, None, :, None] * jnp.transpose(w_uv, (1, 0, 2))[:, :, None, :]
               ).reshape(LAT_WIDTH, MLA_WIDTH).astype(BF16)
    w_a2 = jnp.zeros((LANES, GLA_QK), F32).at[QK_ROPE:QK_ROPE + GLA_GATE_RANK].set(w_gla_a2).astype(BF16)
    row = lambda a: a.reshape(1, -1).astype(F32)
    return dict(g_mix=row(g_mix_norm), w_all=w_all, g_qn=row(g_qnorm), w_qup=w_qup_p, w_uk_bd=w_uk_bd,
                g_kvn=row(g_kvnorm), w_a2=w_a2, b_a=row(b_gla_a), g_gla=row(g_gla_onorm),
                w_uv_bd=w_uv_bd, w_uv_t=jnp.transpose(w_uv, (1, 2, 0)).astype(BF16), w_out=w_out.astype(BF16), w_pg=w_ple_gate.astype(BF16),
                w_pp=w_ple_proj.astype(BF16), g_final=row(g_final))


def _rope_tables(pos):
    inv = 1.0 / (ROPE_THETA ** (jnp.arange(0, QK_ROPE, 2, dtype=F32) / QK_ROPE))
    ang = pos[:, None] * inv[None, :]
    cos, sin = jnp.cos(ang), jnp.sin(ang)
    reps = LANES // QK_ROPE
    return (jnp.tile(jnp.concatenate([cos, cos], axis=1), (1, reps)),
            jnp.tile(jnp.concatenate([-sin, sin], axis=1), (1, reps)))


def _pick_tile(n, pref):
    t = min(n, pref)
    while n % t:
        t //= 2
    return t


def kernel(x_prompt, x_sample, p_prompt, p_sample, cache_ckv, cache_krope, state_gla, page_table, g_mix_norm, w_in, g_qnorm, w_qup, g_kvnorm, w_uk, w_uv, w_gla_a2, b_gla_a, g_gla_onorm, w_out, w_ple_gate, w_ple_proj, g_final):
    n_b, seq, _ = x_prompt.shape
    n_d, n_new, _ = x_sample.shape
    assert w_in.shape[0] == 1, "single-layer stack"
    page = cache_ckv.shape[2]
    past_len = page_table.shape[1] * page
    w = _prep_weights(g_mix_norm[0], w_in[0], g_qnorm[0], w_qup[0], g_kvnorm[0], w_uk[0], w_uv[0],
                      w_gla_a2[0], b_gla_a[0], g_gla_onorm[0], w_out[0], w_ple_gate[0], w_ple_proj[0], g_final)

    ts_p = _pick_tile(seq, 512)
    cos_p, sin_p = _rope_tables(jnp.arange(seq, dtype=F32))
    xp = x_prompt.reshape(n_b * seq, D_MODEL)
    tq = _pick_tile(seq, 512)
    tk = _pick_tile(seq, 512)
    (qt_p, kv_p, ckv_p, kr_p, gm_p, gqk_p, gv_p, la_p, gg_p, ct_p) = _proj_call(
        xp, cos_p, sin_p, w, n_seq=n_b, seq_len=seq, ts=ts_p, q_dtype=BF16, table_period=seq // ts_p, tq=tq)
    omla_p = _mla_prompt_call(qt_p, kv_p.reshape(n_b, seq, QK_LAT), ct_p, w["w_uv_t"], tq=tq, tk=tk)
    chunk_p = min(GLA_CHUNK, seq)
    ogla_p, st_p = _gla_call(gqk_p, gv_p, la_p, w["g_gla"], None, n_seq=n_b, seq_len=seq,
                             rows=_pick_tile(seq, 1024), chunk=chunk_p)
    y_p = _out_call(xp, omla_p.reshape(n_b * seq, MLA_WIDTH), gm_p, ogla_p, gg_p,
                    p_prompt[0].reshape(n_b * seq, P_DIM), w, ts=ts_p)

    n_tok_s = n_d * n_new
    ts_s = _pick_tile(n_tok_s, 512)
    cos_s, sin_s = _rope_tables(past_len + jnp.arange(n_new, dtype=F32))
    cos_s = jnp.tile(cos_s, (ts_s // n_new, 1))
    sin_s = jnp.tile(sin_s, (ts_s // n_new, 1))
    xs = x_sample.reshape(n_tok_s, D_MODEL)
    (q_s, kv_s, ckv_s, kr_s, gm_s, gqk_s, gv_s, la_s, gg_s) = _proj_call(
        xs, cos_s, sin_s, w, n_seq=n_d, seq_len=n_new, ts=ts_s, q_dtype=F32, table_period=1)
    olat_s = _mla_decode_call(q_s, kv_s.reshape(n_d, n_new, QK_LAT), cache_ckv[0],
                              jnp.swapaxes(cache_krope[0], 1, 2), page_table)
    chunk_s = max(n_new, GLA_SAMPLE_CHUNK)
    pad_tok = lambda a: jnp.pad(a.reshape(n_d, n_new, -1), ((0, 0), (0, chunk_s - n_new), (0, 0))
                                ).reshape(n_d * chunk_s, -1)
    ogla_s, st_s = _gla_call(pad_tok(gqk_s), pad_tok(gv_s), pad_tok(la_s), w["g_gla"],
                             state_gla[0].reshape(n_d, GLA_QK, GLA_DV),
                             n_seq=n_d, seq_len=chunk_s, rows=chunk_s, chunk=chunk_s)
    ogla_s = ogla_s.reshape(n_d, chunk_s, GLA_WIDTH)[:, :n_new].reshape(n_tok_s, GLA_WIDTH)
    y_s = _out_call(xs, olat_s.reshape(n_tok_s, LAT_WIDTH), gm_s, ogla_s, gg_s,
                    p_sample[0].reshape(n_tok_s, P_DIM), w, ts=ts_s)

    return (y_p.reshape(n_b, seq, D_MODEL),
            y_s.reshape(n_d, n_new, D_MODEL),
            ckv_p.reshape(1, n_b, seq, KV_LORA),
            kr_p.reshape(1, n_b, seq, QK_ROPE),
            st_p.reshape(1, n_b, GLA_HEADS, GLA_DK, GLA_DV),
            ckv_s.reshape(1, n_d, n_new, KV_LORA),
            kr_s.reshape(1, n_d, n_new, QK_ROPE),
            st_s.reshape(1, n_d, GLA_HEADS, GLA_DK, GLA_DV))
```

```python
import functools

import numpy as np
import jax
import jax.numpy as jnp
from jax import lax
from jax.experimental import pallas as pl
from jax.experimental.pallas import tpu as pltpu

F32 = jnp.float32
BF16 = jnp.bfloat16

D_MODEL = 1024
MLA_HEADS = 8
QK_NOPE = 64
QK_ROPE = 32
V_HEAD = 64
KV_LORA = 128
Q_LORA = 256
MLA_WIDTH = MLA_HEADS * V_HEAD
GLA_HEADS = 4
GLA_DK = 64
GLA_DV = 128
GLA_WIDTH = GLA_HEADS * GLA_DV
GLA_QK = GLA_HEADS * GLA_DK
GLA_GATE_RANK = 16
GLA_TAU = 16.0
P_DIM = 256
NORM_EPS = 1e-6
ROPE_THETA = 10000.0
QK_LAT = KV_LORA + QK_ROPE
LAT_WIDTH = MLA_HEADS * KV_LORA
Z_WIDTH = 2560
GLA_CHUNK = 64
LOG2_E = 1.4426950408889634
GLA_SAMPLE_CHUNK = 16
GLA_SAMPLE_SEQS = 8
GLA_UNROLL = 4
DECODE_CHUNK_KEYS = 4096
MLA_TILES_PER_STEP = 2
MLA_CHUNK_COLS = 1024
ONES_ROWS = 16
DMA_LOOP_UNROLL = 8

LANES = 128
VMEM_LIMIT = 56 * 1024 * 1024


def _rms(x):
    return x * lax.rsqrt(jnp.mean(x * x, axis=-1, keepdims=True) + NORM_EPS)


def _silu(x):
    return x * (1.0 / (1.0 + jnp.exp(-x)))


def _swap16(x):
    n = x.shape[-1]
    lane = lax.broadcasted_iota(jnp.int32, x.shape, x.ndim - 1)
    fwd = pltpu.roll(x, n - 16, axis=x.ndim - 1)
    bwd = pltpu.roll(x, 16, axis=x.ndim - 1)
    return jnp.where((lane & 31) < 16, fwd, bwd)


def _absorb_kernel(wq_nope_ref, wuk_bd_ref, wq_rope_ref, o_ref):
    o_ref[:, :LAT_WIDTH] = jnp.dot(wq_nope_ref[...], wuk_bd_ref[...], preferred_element_type=F32,
                                   precision=lax.Precision.HIGHEST).astype(o_ref.dtype)
    o_ref[:, LAT_WIDTH:] = wq_rope_ref[...].astype(o_ref.dtype)


def _absorb_call(wq_nope, wuk_bd, wq_rope):
    return pl.pallas_call(
        _absorb_kernel,
        out_shape=jax.ShapeDtypeStruct((Q_LORA, LAT_WIDTH + MLA_HEADS * QK_ROPE), BF16),
        compiler_params=pltpu.CompilerParams(vmem_limit_bytes=VMEM_LIMIT),
        name="absorb",
    )(wq_nope, wuk_bd, wq_rope)


def _proj_kernel(x_ref, cos_ref, sin_ref, gmix_ref, wall_ref, gqn_ref, wq_ref,
                 gkvn_ref, wa2_ref, ba_ref,
                 q_out, kv_out, ckv_out, kr_out, gm_out, gqk_out, gv_out, la_out, gg_out, ct_out=None,
                 *, scale, tq):
    x = x_ref[...]
    xn = (_rms(x) * gmix_ref[...]).astype(BF16)
    z = jnp.dot(xn, wall_ref[...], preferred_element_type=F32)
    cos = cos_ref[...]
    sin = sin_ref[...]

    qn = (_rms(z[:, 0:Q_LORA]) * gqn_ref[...]).astype(BF16)
    q = jnp.dot(qn, wq_ref[...], preferred_element_type=F32)
    q_lat = q[:, :LAT_WIDTH] * scale
    qr = q[:, LAT_WIDTH:]
    cos2 = jnp.concatenate([cos, cos], axis=1)
    sin2 = jnp.concatenate([sin, sin], axis=1)
    qr = (qr * cos2 + _swap16(qr) * sin2) * scale
    if tq is None:
        nb, _, lb, _ = q_out.shape
        for h in range(MLA_HEADS):
            q_h = jnp.concatenate([q_lat[:, h * KV_LORA:(h + 1) * KV_LORA],
                                   qr[:, h * QK_ROPE:(h + 1) * QK_ROPE]], axis=1)
            q_out[:, h, :, :] = q_h.reshape(nb, lb, QK_LAT).astype(q_out.dtype)
    else:
        q_lat_t = jnp.transpose(q_lat).astype(q_out.dtype)
        qr_t = jnp.transpose(qr).astype(q_out.dtype)
        for h in range(MLA_HEADS):
            for j in range(q_out.shape[1]):
                tok = slice(j * tq, (j + 1) * tq)
                q_out[0, j, 0:KV_LORA, h * tq:(h + 1) * tq] = q_lat_t[h * KV_LORA:(h + 1) * KV_LORA, tok]
                q_out[0, j, KV_LORA:QK_LAT, h * tq:(h + 1) * tq] = qr_t[h * QK_ROPE:(h + 1) * QK_ROPE, tok]

    ckv = _rms(z[:, 256:384]) * gkvn_ref[...]
    ckv_out[...] = ckv
    blk = z[:, 384:512]
    kr = (blk * cos + _swap16(blk) * sin)[:, :QK_ROPE]
    kr_out[...] = kr
    kv_out[:, 0:KV_LORA] = ckv.astype(kv_out.dtype)
    kv_out[:, KV_LORA:QK_LAT] = kr.astype(kv_out.dtype)
    if tq is not None:
        ct_out[0] = jnp.transpose(ckv).astype(ct_out.dtype)

    a_pre = jnp.dot(blk.astype(BF16), wa2_ref[...], preferred_element_type=F32) + ba_ref[...]
    la_out[...] = (jnp.minimum(a_pre, 0.0) - jnp.log1p(jnp.exp(-jnp.abs(a_pre)))) * (1.0 / GLA_TAU)

    gm_out[...] = _silu(z[:, 512:1024]).astype(gm_out.dtype)
    gqk_out[...] = z[:, 1024:1536]
    gv_out[...] = z[:, 1536:2048].astype(gv_out.dtype)
    gg_out[...] = _silu(z[:, 2048:2560]).astype(gg_out.dtype)


def _proj_call(x2d, cos_t, sin_t, w, *, n_seq, seq_len, ts, q_dtype, table_period, tq=None):
    n_tok = x2d.shape[0]
    n_tiles = n_tok // ts
    tok = lambda width: pl.BlockSpec((ts, width), lambda i: (i, 0))
    full = lambda a: pl.BlockSpec(a.shape, lambda i: (0,) * a.ndim)
    tab = pl.BlockSpec((ts, LANES), lambda i: (i % table_period, 0))
    if tq is None:
        assert ts % seq_len == 0
        q_shape = jax.ShapeDtypeStruct((n_seq, MLA_HEADS, seq_len, QK_LAT), q_dtype)
        q_spec = pl.BlockSpec((ts // seq_len, MLA_HEADS, seq_len, QK_LAT), lambda i: (i, 0, 0, 0))
        extra_shape, extra_spec = (), ()
    else:
        assert seq_len % ts == 0 and ts % tq == 0
        per = seq_len // ts
        q_shape = jax.ShapeDtypeStruct((n_seq, seq_len // tq, QK_LAT, MLA_HEADS * tq), q_dtype)
        q_spec = pl.BlockSpec((1, ts // tq, QK_LAT, MLA_HEADS * tq), lambda i: (i // per, i % per, 0, 0))
        extra_shape = (jax.ShapeDtypeStruct((n_seq, KV_LORA, seq_len), q_dtype),)
        extra_spec = (pl.BlockSpec((1, KV_LORA, ts), lambda i: (i // per, 0, i % per)),)
    out_shape = (
        q_shape,
        jax.ShapeDtypeStruct((n_tok, QK_LAT), q_dtype),
        jax.ShapeDtypeStruct((n_tok, KV_LORA), F32),
        jax.ShapeDtypeStruct((n_tok, QK_ROPE), F32),
        jax.ShapeDtypeStruct((n_tok, MLA_WIDTH), BF16),
        jax.ShapeDtypeStruct((n_tok, 2 * GLA_QK), F32),
        jax.ShapeDtypeStruct((n_tok, GLA_WIDTH), BF16),
        jax.ShapeDtypeStruct((n_tok, GLA_QK), F32),
        jax.ShapeDtypeStruct((n_tok, GLA_WIDTH), BF16),
    ) + extra_shape
    out_specs = (
        q_spec,
        tok(QK_LAT), tok(KV_LORA), tok(QK_ROPE), tok(MLA_WIDTH), tok(2 * GLA_QK),
        tok(GLA_WIDTH), tok(GLA_QK), tok(GLA_WIDTH),
    ) + extra_spec
    weights = (w["g_mix"], w["w_all"], w["g_qn"], w["w_q"], w["g_kvn"],
               w["w_a2"], w["b_a"])
    return pl.pallas_call(
        functools.partial(_proj_kernel, scale=(QK_NOPE + QK_ROPE) ** -0.5 * LOG2_E, tq=tq),
        grid=(n_tiles,),
        in_specs=[tok(D_MODEL), tab, tab] + [full(a) for a in weights],
        out_specs=out_specs,
        out_shape=out_shape,
        compiler_params=pltpu.CompilerParams(dimension_semantics=("arbitrary",),
                                             vmem_limit_bytes=VMEM_LIMIT),
        name="proj",
    )(x2d, cos_t, sin_t, *weights)


def _mla_prompt_kernel(qi_ref, kj_ref, qt_ref, kv_ref, ct_ref, wuvt_ref, o_ref, m_sc, acc_sc, s_buf, p_buf,
                       *, tq, tk, cw, n_sub):
    step = pl.program_id(1)
    qi = qi_ref[step]
    kg = kj_ref[step]
    last_kj = (qi * tq + (tq - 1)) // tk
    n_cols = MLA_HEADS * tq

    @pl.when(kg == 0)
    def _():
        m_sc[...] = jnp.full(m_sc.shape, -jnp.inf, F32)
        acc_sc[...] = jnp.zeros(acc_sc.shape, F32)

    def process(sub, kj, masked):
        kv = kv_ref[0, sub * tk:(sub + 1) * tk, :]
        ct = jnp.concatenate([ct_ref[0, :, sub * tk:(sub + 1) * tk], jnp.ones((ONES_ROWS, tk), BF16)], axis=0)
        n_chunks = n_cols // cw

        def scores(c):
            cols = slice(c * cw, (c + 1) * cw)
            s = jnp.dot(kv, qt_ref[0, 0, :, cols], preferred_element_type=F32)
            if masked:
                kpos = kj * tk + lax.broadcasted_iota(jnp.int32, (tk, cw), 0)
                qpos = qi * tq + ((c * cw + lax.broadcasted_iota(jnp.int32, (tk, cw), 1)) & (tq - 1))
                s = jnp.where(kpos <= qpos, s, -jnp.inf)
            s_buf[c % 3] = s
            m_prev = m_sc[:, cols]
            m_new = jnp.maximum(m_prev, jnp.max(s, axis=0, keepdims=True))
            m_sc[:, cols] = m_new
            return m_new, jnp.exp2(m_prev - m_new)

        def probs(c, m_new):
            p_buf[c % 2] = jnp.exp2((s_buf[c % 3] - m_new).astype(BF16))

        def values(c, alpha):
            cols = slice(c * cw, (c + 1) * cw)
            acc_sc[:, cols] = alpha * acc_sc[:, cols] + jnp.dot(ct, p_buf[c % 2], preferred_element_type=F32)

        stats = {}
        for t in range(n_chunks + 3):
            if t < n_chunks:
                stats[t] = scores(t)
            if 0 <= t - 2 < n_chunks:
                probs(t - 2, stats[t - 2][0])
            if 0 <= t - 3 < n_chunks:
                values(t - 3, stats[t - 3][1])

    for sub in range(n_sub):
        kj = kg * n_sub + sub

        @pl.when(kj < last_kj)
        def _():
            process(sub, kj, False)

        @pl.when(kj == last_kj)
        def _():
            process(sub, kj, True)

    @pl.when(kg == last_kj // n_sub)
    def _():
        for pair in range(MLA_HEADS // 2):
            halves = []
            for h in (2 * pair, 2 * pair + 1):
                cols = slice(h * tq, (h + 1) * tq)
                o_lat_t = acc_sc[0:KV_LORA, cols] * (1.0 / acc_sc[KV_LORA:KV_LORA + 1, cols])
                halves.append(jnp.dot(wuvt_ref[h], o_lat_t.astype(BF16), preferred_element_type=F32))
            o_ref[0, :, pair * LANES:(pair + 1) * LANES] = jnp.transpose(
                jnp.concatenate(halves, axis=0)).astype(o_ref.dtype)


def _mla_prompt_call(qt, kv, ct, w_uv_t, *, tq, tk):
    n_b, n_qb, _, n_cols = qt.shape
    seq = n_qb * tq
    assert tq & (tq - 1) == 0
    cw = min(n_cols, MLA_CHUNK_COLS)
    n_sub = MLA_TILES_PER_STEP if seq % (MLA_TILES_PER_STEP * tk) == 0 else 1
    qi_l, kj_l = [], []
    for i in range(n_qb):
        for g in range(((i * tq + tq - 1) // tk) // n_sub + 1):
            qi_l.append(i)
            kj_l.append(g)
    qi_t = jnp.asarray(np.array(qi_l, np.int32))
    kj_t = jnp.asarray(np.array(kj_l, np.int32))
    grid_spec = pltpu.PrefetchScalarGridSpec(
        num_scalar_prefetch=2,
        grid=(n_b, len(qi_l)),
        in_specs=[
            pl.BlockSpec((1, 1, QK_LAT, n_cols), lambda b, s, qi, kj: (b, qi[s], 0, 0)),
            pl.BlockSpec((1, n_sub * tk, QK_LAT), lambda b, s, qi, kj: (b, kj[s], 0)),
            pl.BlockSpec((1, KV_LORA, n_sub * tk), lambda b, s, qi, kj: (b, 0, kj[s])),
            pl.BlockSpec(w_uv_t.shape, lambda b, s, qi, kj: (0, 0, 0)),
        ],
        out_specs=pl.BlockSpec((1, tq, MLA_WIDTH), lambda b, s, qi, kj: (b, qi[s], 0)),
        scratch_shapes=[pltpu.VMEM((1, n_cols), F32), pltpu.VMEM((KV_LORA + ONES_ROWS, n_cols), F32),
                        pltpu.VMEM((3, tk, cw), F32), pltpu.VMEM((2, tk, cw), BF16)],
    )
    return pl.pallas_call(
        functools.partial(_mla_prompt_kernel, tq=tq, tk=tk, cw=cw, n_sub=n_sub),
        grid_spec=grid_spec,
        out_shape=jax.ShapeDtypeStruct((n_b, seq, MLA_WIDTH), BF16),
        compiler_params=pltpu.CompilerParams(dimension_semantics=("arbitrary", "arbitrary"),
                                             vmem_limit_bytes=VMEM_LIMIT),
        name="mla_prompt",
    )(qi_t, kj_t, qt, kv, ct, w_uv_t)


def _mla_decode_kernel(pt_ref, q_ref, kvn_ref, lat_hbm, krt_hbm, o_ref,
                       lat_buf, krt_buf, sems, *, n_pages, page, n_new, kc):
    b = pl.program_id(0)
    slot = b % 2
    rows = MLA_HEADS * n_new

    def page_copies(seq, slot_, i):
        pg = pt_ref[seq * n_pages + i]
        off = i * page if isinstance(i, int) else pl.multiple_of(i * page, page)
        return (pltpu.make_async_copy(lat_hbm.at[pg], lat_buf.at[slot_, pl.ds(off, page), :], sems.at[0, slot_]),
                pltpu.make_async_copy(krt_hbm.at[pg], krt_buf.at[slot_, i], sems.at[1, slot_]))

    def for_pages(seq, slot_, lo, hi, action):
        def body(i, carry):
            for cp in page_copies(seq, slot_, i):
                action(cp)
            return carry
        lax.fori_loop(lo, hi, body, 0, unroll=DMA_LOOP_UNROLL)

    start = lambda cp: cp.start()
    wait = lambda cp: cp.wait()
    last = pl.num_programs(0) - 1

    @pl.when(b == 0)
    def _():
        for_pages(0, 0, 0, n_pages, start)

    for_pages(b, slot, 0, n_pages, wait)
    nxt = jnp.minimum(b + 1, last)
    n_kc = n_pages * page // kc

    q = q_ref[0].reshape(rows, QK_LAT).astype(BF16)
    q_lat = q[:, :KV_LORA]
    q_rope = q[:, KV_LORA:]
    contract_last = (((1,), (1,)), ((), ()))
    lats, scores = [], []
    for c in range(n_kc):
        for i in range(c * n_pages // n_kc, (c + 1) * n_pages // n_kc):
            for cp in page_copies(nxt, 1 - slot, i):
                cp.start()
        lat = lat_buf[slot, c * kc:(c + 1) * kc, :].astype(BF16)
        krt = jnp.concatenate([krt_buf[slot, i] for i in range(c * kc // page, (c + 1) * kc // page)],
                              axis=1).astype(BF16)
        lats.append(lat)
        scores.append(lax.dot_general(q_lat, lat, contract_last, preferred_element_type=F32)
                      + jnp.dot(q_rope, krt, preferred_element_type=F32))
    kn = jnp.concatenate([kvn_ref[0], jnp.zeros((LANES - n_new, QK_LAT), F32)], axis=0).astype(BF16)
    sn = lax.dot_general(q, kn, contract_last, preferred_element_type=F32)
    tpos = lax.broadcasted_iota(jnp.int32, (rows, LANES), 0) & (n_new - 1)
    jpos = lax.broadcasted_iota(jnp.int32, (rows, LANES), 1)
    lats.append(kn[:, :KV_LORA])
    scores.append(jnp.where(jpos <= tpos, sn, -jnp.inf))

    m = functools.reduce(jnp.maximum, [jnp.max(s, axis=1, keepdims=True) for s in scores])
    probs = [jnp.exp2(s - m) for s in scores]
    l = functools.reduce(jnp.add, [jnp.sum(p, axis=1, keepdims=True) for p in probs])
    acc = functools.reduce(jnp.add, [jnp.dot(p.astype(BF16), v, preferred_element_type=F32)
                                     for p, v in zip(probs, lats)])
    o = acc * (1.0 / l)
    for h in range(MLA_HEADS):
        o_ref[0, :, h * KV_LORA:(h + 1) * KV_LORA] = o[h * n_new:(h + 1) * n_new, :].astype(o_ref.dtype)

    @pl.when(b == last)
    def _():
        for_pages(nxt, 1 - slot, 0, n_pages, wait)


def _mla_decode_call(q, kv_new, lat_pool, krt_pool, page_table):
    n_b, _, n_new, _ = q.shape
    assert n_new & (n_new - 1) == 0 and n_new <= LANES
    n_pages = page_table.shape[1]
    page = lat_pool.shape[1]
    past = n_pages * page
    kc = _pick_tile(past, DECODE_CHUNK_KEYS)
    grid_spec = pltpu.PrefetchScalarGridSpec(
        num_scalar_prefetch=1,
        grid=(n_b,),
        in_specs=[pl.BlockSpec((1, MLA_HEADS, n_new, QK_LAT), lambda b, pt: (b, 0, 0, 0)),
                  pl.BlockSpec((1, n_new, QK_LAT), lambda b, pt: (b, 0, 0)),
                  pl.BlockSpec(memory_space=pl.ANY),
                  pl.BlockSpec(memory_space=pl.ANY)],
        out_specs=pl.BlockSpec((1, n_new, LAT_WIDTH), lambda b, pt: (b, 0, 0)),
        scratch_shapes=[pltpu.VMEM((2, past, KV_LORA), F32), pltpu.VMEM((2, n_pages, QK_ROPE, page), F32),
                        pltpu.SemaphoreType.DMA((2, 2))],
    )
    return pl.pallas_call(
        functools.partial(_mla_decode_kernel, n_pages=n_pages, page=page, n_new=n_new, kc=kc),
        grid_spec=grid_spec,
        out_shape=jax.ShapeDtypeStruct((n_b, n_new, LAT_WIDTH), BF16),
        compiler_params=pltpu.CompilerParams(dimension_semantics=("arbitrary",),
                                             vmem_limit_bytes=VMEM_LIMIT),
        name="mla_decode",
    )(page_table.reshape(-1), q, kv_new, lat_pool, krt_pool)


def _gla_levels(chunk):
    levels = []
    h = chunk // 2
    while h >= 1:
        levels.append(h)
        h //= 2
    return levels


def _gla_decay_matrix(chunk):
    t = np.arange(chunk)[:, None]
    u = np.arange(chunk)[None, :]
    blocks = [(u <= t), (u > t)]
    for h in _gla_levels(chunk):
        off = t % (2 * h)
        mid = t - off + h
        upper = (off >= h) & (u >= mid) & (u <= t)
        lower = (off < h) & (u > t) & (u < mid)
        blocks.append(upper | lower)
    return np.concatenate(blocks, axis=0).astype(np.float32)


def _gla_chunk(qk, v, la, g_mat, gn, s_in, s_out, o_ref, r0, *, chunk):
    levels = _gla_levels(chunk)
    la_hi = la.astype(BF16)
    la_lo = (la - la_hi.astype(F32)).astype(BF16)
    e_all = jnp.dot(g_mat, jnp.concatenate([la_hi, la_lo], axis=0),
                    preferred_element_type=F32)
    q = qk[:, :GLA_QK] * (GLA_DK ** -0.5)
    k = qk[:, GLA_QK:]
    e_cum = e_all[0:chunk]
    q_in = q * jnp.exp(e_cum)
    k_out = k * jnp.exp(e_all[chunk:2 * chunk])
    row = lax.broadcasted_iota(jnp.int32, (chunk, GLA_QK), 0)
    q_lv, k_lv = [q], [k]
    for li, h in enumerate(levels):
        w = jnp.exp(e_all[(2 + li) * chunk:(3 + li) * chunk])
        upper = (row & (2 * h - 1)) >= h
        q_lv.append(jnp.where(upper, q * w, 0.0))
        k_lv.append(jnp.where(upper, 0.0, k * w))
    contract_last = (((1,), (1,)), ((), ()))
    contract_first = (((0,), (0,)), ((), ()))
    head_of_lane = lax.broadcasted_iota(jnp.int32, (chunk, GLA_QK), 1) // GLA_DK
    stack_heads = lambda x: jnp.concatenate(
        [jnp.where(head_of_lane == hd, x, 0.0) for hd in range(GLA_HEADS)], axis=0).astype(BF16)
    trow = lax.broadcasted_iota(jnp.int32, (GLA_HEADS * chunk, chunk), 0) & (chunk - 1)
    tcol = lax.broadcasted_iota(jnp.int32, (GLA_HEADS * chunk, chunk), 1)
    masks = [trow == tcol] + [(trow // (2 * h)) == (tcol // (2 * h)) for h in levels]
    a_all = jnp.zeros((GLA_HEADS * chunk, chunk), F32)
    for ql, kl, mk in zip(q_lv, k_lv, masks):
        a_all = a_all + jnp.where(mk, lax.dot_general(stack_heads(ql), kl.astype(BF16), contract_last,
                                                      preferred_element_type=F32), 0.0)
    a_all = a_all.astype(BF16)
    lane = lax.broadcasted_iota(jnp.int32, (chunk, LANES), 1)
    decay_row = e_cum[chunk - 1:chunk, :]
    for p in range(GLA_HEADS // 2):
        ls = slice(p * LANES, (p + 1) * LANES)
        s_pair = s_in[ls, :]
        decay = jnp.exp(jnp.transpose(jnp.broadcast_to(decay_row[:, ls], (LANES, LANES))))
        k_out_pair = k_out[:, ls].astype(BF16)
        new_rows = []
        for half in range(2):
            hd = 2 * p + half
            in_head = (lane // GLA_DK) == half
            v_h = v[:, hd * GLA_DV:(hd + 1) * GLA_DV]
            qi = jnp.where(in_head, q_in[:, ls], 0.0).astype(BF16)
            lhs = jnp.concatenate([qi, a_all[hd * chunk:(hd + 1) * chunk]], axis=1)
            rhs = jnp.concatenate([s_pair.astype(BF16), v_h], axis=0)
            o = jnp.dot(lhs, rhs, preferred_element_type=F32)
            o_ref[pl.ds(r0, chunk), hd * GLA_DV:(hd + 1) * GLA_DV] = (_rms(o) * gn).astype(o_ref.dtype)
            upd = lax.dot_general(k_out_pair, v_h, contract_first, preferred_element_type=F32)
            hr = slice(half * GLA_DK, (half + 1) * GLA_DK)
            new_rows.append(decay[hr, :] * s_pair[hr, :] + upd[hr, :])
        s_out[ls, :] = jnp.concatenate(new_rows, axis=0)


def _gla_kernel(*refs, chunk, n_sub, independent):
    if independent:
        qk_ref, v_ref, la_ref, g_ref, gn_ref, s0_ref, o_ref, st_ref = refs
    else:
        qk_ref, v_ref, la_ref, g_ref, gn_ref, o_ref, st_ref, s_sc = refs
        j = pl.program_id(1)

        @pl.when(j == 0)
        def _():
            s_sc[...] = jnp.zeros(s_sc.shape, F32)

    g_mat = g_ref[...]
    gn = gn_ref[...]

    def body(c, carry):
        r0 = pl.multiple_of(c * chunk, chunk)
        s_in, s_out = (s0_ref.at[c], st_ref.at[c]) if independent else (s_sc, s_sc)
        _gla_chunk(qk_ref[pl.ds(r0, chunk), :], v_ref[pl.ds(r0, chunk), :], la_ref[pl.ds(r0, chunk), :],
                   g_mat, gn, s_in, s_out, o_ref, r0, chunk=chunk)
        return carry

    lax.fori_loop(0, n_sub, body, 0, unroll=min(n_sub, GLA_UNROLL))

    if not independent:
        @pl.when(j == pl.num_programs(1) - 1)
        def _():
            st_ref[0] = s_sc[...]


def _gla_call(qk, v, la, gn, s0, *, n_seq, seq_len, rows, chunk):
    independent = s0 is not None
    g_np = _gla_decay_matrix(chunk)
    g_mat = jnp.asarray(np.concatenate([g_np, g_np], axis=1), BF16)
    full = lambda a: pl.BlockSpec(a.shape, lambda b, j: (0,) * a.ndim)
    if independent:
        assert seq_len == chunk and (n_seq * chunk) % rows == 0
        grid = (n_seq * chunk // rows, 1)
        tok = lambda width: pl.BlockSpec((rows, width), lambda b, j: (b, 0))
        state = pl.BlockSpec((rows // chunk, GLA_QK, GLA_DV), lambda b, j: (b, 0, 0))
        args, extra_specs, scratch = [qk, v, la, g_mat, gn, s0], [state], []
    else:
        per = seq_len // rows
        grid = (n_seq, per)
        tok = lambda width: pl.BlockSpec((rows, width), lambda b, j: (b * per + j, 0))
        state = pl.BlockSpec((1, GLA_QK, GLA_DV), lambda b, j: (b, 0, 0))
        args, extra_specs, scratch = [qk, v, la, g_mat, gn], [], [pltpu.VMEM((GLA_QK, GLA_DV), F32)]
    return pl.pallas_call(
        functools.partial(_gla_kernel, chunk=chunk, n_sub=rows // chunk, independent=independent),
        grid=grid,
        in_specs=[tok(2 * GLA_QK), tok(GLA_WIDTH), tok(GLA_QK), full(g_mat), full(gn)] + extra_specs,
        out_specs=(tok(GLA_WIDTH), state),
        out_shape=(jax.ShapeDtypeStruct((n_seq * seq_len, GLA_WIDTH), BF16),
                   jax.ShapeDtypeStruct((n_seq, GLA_QK, GLA_DV), F32)),
        scratch_shapes=scratch,
        compiler_params=pltpu.CompilerParams(dimension_semantics=("arbitrary", "arbitrary"),
                                             vmem_limit_bytes=VMEM_LIMIT),
        name="gla",
    )(*args)


def _out_kernel(x_ref, omla_ref, gm_ref, ogla_ref, gg_ref, p_ref, wuv_ref, wout_ref, wpg_ref, wpp_ref,
                gfin_ref, y_ref, *, latent_in):
    if latent_in:
        o_mla = jnp.dot(omla_ref[...], wuv_ref[...], preferred_element_type=F32)
    else:
        o_mla = omla_ref[...].astype(F32)
    o_mla = o_mla * gm_ref[...].astype(F32)
    o_gla = ogla_ref[...].astype(F32) * gg_ref[...].astype(F32)
    mix = jnp.concatenate([o_mla, o_gla], axis=1).astype(BF16)
    h = x_ref[...] + jnp.dot(mix, wout_ref[...], preferred_element_type=F32)
    gate = jnp.dot(h.astype(BF16), wpg_ref[...], preferred_element_type=F32)
    gate = 1.0 / (1.0 + jnp.exp(-gate))
    emb = jnp.dot(p_ref[...].astype(BF16), wpp_ref[...], preferred_element_type=F32)
    h = h + gate * emb
    y_ref[...] = _rms(h) * gfin_ref[...]


def _out_call(x2d, o_mla, gm, o_gla, gg, p2d, w, *, ts):
    n_tok = x2d.shape[0]
    mla_w = o_mla.shape[1]
    tok = lambda width: pl.BlockSpec((ts, width), lambda i: (i, 0))
    full = lambda a: pl.BlockSpec(a.shape, lambda i: (0,) * a.ndim)
    weights = (w["w_uv_bd"], w["w_out"], w["w_pg"], w["w_pp"], w["g_final"])
    return pl.pallas_call(
        functools.partial(_out_kernel, latent_in=(mla_w == LAT_WIDTH)),
        grid=(n_tok // ts,),
        in_specs=[tok(D_MODEL), tok(mla_w), tok(MLA_WIDTH), tok(GLA_WIDTH), tok(GLA_WIDTH), tok(P_DIM)]
        + [full(a) for a in weights],
        out_specs=tok(D_MODEL),
        out_shape=jax.ShapeDtypeStruct((n_tok, D_MODEL), F32),
        compiler_params=pltpu.CompilerParams(dimension_semantics=("arbitrary",),
                                             vmem_limit_bytes=VMEM_LIMIT),
        name="out",
    )(x2d, o_mla, gm, o_gla, gg, p2d, *weights)


def _prep_weights(g_mix_norm, w_in, g_qnorm, w_qup, g_kvnorm, w_uk, w_uv, w_gla_a2, b_gla_a, g_gla_onorm,
                  w_out, w_ple_gate, w_ple_proj, g_final):
    o_kv = Q_LORA
    o_gm = o_kv + KV_LORA + QK_ROPE
    o_gq = o_gm + MLA_WIDTH
    o_gk = o_gq + GLA_QK
    o_gv = o_gk + GLA_QK
    o_a = o_gv + GLA_WIDTH
    o_gg = o_a + GLA_GATE_RANK
    pad = jnp.zeros((D_MODEL, 512 - (Q_LORA + KV_LORA + QK_ROPE + GLA_GATE_RANK)), F32)
    w_all = jnp.concatenate([
        w_in[:, 0:o_gm], w_in[:, o_a:o_gg], pad,
        w_in[:, o_gm:o_gq],
        w_in[:, o_gq:o_gv],
        w_in[:, o_gv:o_a],
        w_in[:, o_gg:],
    ], axis=1).astype(BF16)
    assert w_all.shape == (D_MODEL, Z_WIDTH)
    w_qup3 = w_qup.reshape(Q_LORA, MLA_HEADS, QK_NOPE + QK_ROPE)
    eye = jnp.eye(MLA_HEADS, dtype=F32)
    w_uk_bd = (eye[:, None, :, None] * jnp.transpose(w_uk, (1, 2, 0))[:, :, None, :]
               ).reshape(MLA_HEADS * QK_NOPE, LAT_WIDTH)
    w_q = _absorb_call(w_qup3[:, :, :QK_NOPE].reshape(Q_LORA, MLA_HEADS * QK_NOPE), w_uk_bd,
                       w_qup3[:, :, QK_NOPE:].reshape(Q_LORA, MLA_HEADS * QK_ROPE))
    w_uv_bd =(eye[:, None, :, None] * jnp.transpose(w_uv, (1, 0, 2))[:, :, None, :]
               ).reshape(LAT_WIDTH, MLA_WIDTH).astype(BF16)
    w_a2 = jnp.zeros((LANES, GLA_QK), F32).at[QK_ROPE:QK_ROPE + GLA_GATE_RANK].set(w_gla_a2).astype(BF16)
    row = lambda a: a.reshape(1, -1).astype(F32)
    return dict(g_mix=row(g_mix_norm), w_all=w_all, g_qn=row(g_qnorm), w_q=w_q,
                g_kvn=row(g_kvnorm), w_a2=w_a2, b_a=row(b_gla_a), g_gla=row(g_gla_onorm),
                w_uv_bd=w_uv_bd, w_uv_t=jnp.transpose(w_uv, (1, 2, 0)).astype(BF16), w_out=w_out.astype(BF16), w_pg=w_ple_gate.astype(BF16),
                w_pp=w_ple_proj.astype(BF16), g_final=row(g_final))


def _rope_tables(pos):
    inv = 1.0 / (ROPE_THETA ** (jnp.arange(0, QK_ROPE, 2, dtype=F32) / QK_ROPE))
    ang = pos[:, None] * inv[None, :]
    cos, sin = jnp.cos(ang), jnp.sin(ang)
    reps = LANES // QK_ROPE
    return (jnp.tile(jnp.concatenate([cos, cos], axis=1), (1, reps)),
            jnp.tile(jnp.concatenate([-sin, sin], axis=1), (1, reps)))


def _pick_tile(n, pref):
    t = min(n, pref)
    while n % t:
        t //= 2
    return t


def kernel(x_prompt, x_sample, p_prompt, p_sample, cache_ckv, cache_krope, state_gla, page_table, g_mix_norm, w_in, g_qnorm, w_qup, g_kvnorm, w_uk, w_uv, w_gla_a2, b_gla_a, g_gla_onorm, w_out, w_ple_gate, w_ple_proj, g_final):
    n_b, seq, _ = x_prompt.shape
    n_d, n_new, _ = x_sample.shape
    assert w_in.shape[0] == 1, "single-layer stack"
    page = cache_ckv.shape[2]
    past_len = page_table.shape[1] * page
    w = _prep_weights(g_mix_norm[0], w_in[0], g_qnorm[0], w_qup[0], g_kvnorm[0], w_uk[0], w_uv[0],
                      w_gla_a2[0], b_gla_a[0], g_gla_onorm[0], w_out[0], w_ple_gate[0], w_ple_proj[0], g_final)

    ts_p = _pick_tile(seq, 512)
    cos_p, sin_p = _rope_tables(jnp.arange(seq, dtype=F32))
    xp = x_prompt.reshape(n_b * seq, D_MODEL)
    tq = _pick_tile(seq, 512)
    tk = _pick_tile(seq, 512)
    (qt_p, kv_p, ckv_p, kr_p, gm_p, gqk_p, gv_p, la_p, gg_p, ct_p) = _proj_call(
        xp, cos_p, sin_p, w, n_seq=n_b, seq_len=seq, ts=ts_p, q_dtype=BF16, table_period=seq // ts_p, tq=tq)
    omla_p = _mla_prompt_call(qt_p, kv_p.reshape(n_b, seq, QK_LAT), ct_p, w["w_uv_t"], tq=tq, tk=tk)
    chunk_p = min(GLA_CHUNK, seq)
    ogla_p, st_p = _gla_call(gqk_p, gv_p, la_p, w["g_gla"], None, n_seq=n_b, seq_len=seq,
                             rows=_pick_tile(seq, 1024), chunk=chunk_p)
    y_p = _out_call(xp, omla_p.reshape(n_b * seq, MLA_WIDTH), gm_p, ogla_p, gg_p,
                    p_prompt[0].reshape(n_b * seq, P_DIM), w, ts=ts_p)

    n_tok_s = n_d * n_new
    ts_s = _pick_tile(n_tok_s, 512)
    cos_s, sin_s = _rope_tables(past_len + jnp.arange(n_new, dtype=F32))
    cos_s = jnp.tile(cos_s, (ts_s // n_new, 1))
    sin_s = jnp.tile(sin_s, (ts_s // n_new, 1))
    xs = x_sample.reshape(n_tok_s, D_MODEL)
    (q_s, kv_s, ckv_s, kr_s, gm_s, gqk_s, gv_s, la_s, gg_s) = _proj_call(
        xs, cos_s, sin_s, w, n_seq=n_d, seq_len=n_new, ts=ts_s, q_dtype=F32, table_period=1)
    olat_s = _mla_decode_call(q_s, kv_s.reshape(n_d, n_new, QK_LAT), cache_ckv[0],
                              jnp.swapaxes(cache_krope[0], 1, 2), page_table)
    chunk_s = max(n_new, GLA_SAMPLE_CHUNK)
    pad_tok = lambda a: jnp.pad(a.reshape(n_d, n_new, -1), ((0, 0), (0, chunk_s - n_new), (0, 0))
                                ).reshape(n_d * chunk_s, -1)
    ogla_s, st_s = _gla_call(pad_tok(gqk_s), pad_tok(gv_s), pad_tok(la_s), w["g_gla"],
                             state_gla[0].reshape(n_d, GLA_QK, GLA_DV),
                             n_seq=n_d, seq_len=chunk_s, rows=chunk_s * _pick_tile(n_d, GLA_SAMPLE_SEQS),
                             chunk=chunk_s)
    ogla_s = ogla_s.reshape(n_d, chunk_s, GLA_WIDTH)[:, :n_new].reshape(n_tok_s, GLA_WIDTH)
    y_s = _out_call(xs, olat_s.reshape(n_tok_s, LAT_WIDTH), gm_s, ogla_s, gg_s,
                    p_sample[0].reshape(n_tok_s, P_DIM), w, ts=ts_s)

    return (y_p.reshape(n_b, seq, D_MODEL),
            y_s.reshape(n_d, n_new, D_MODEL),
            ckv_p.reshape(1, n_b, seq, KV_LORA),
            kr_p.reshape(1, n_b, seq, QK_ROPE),
            st_p.reshape(1, n_b, GLA_HEADS, GLA_DK, GLA_DV),
            ckv_s.reshape(1, n_d, n_new, KV_LORA),
            kr_s.reshape(1, n_d, n_new, QK_ROPE),
            st_s.reshape(1, n_d, GLA_HEADS, GLA_DK, GLA_DV))
```

```python
import functools

import numpy as np
import jax
import jax.numpy as jnp
from jax import lax
from jax.experimental import pallas as pl
from jax.experimental.pallas import tpu as pltpu

F32 = jnp.float32
BF16 = jnp.bfloat16

D_MODEL = 1024
MLA_HEADS = 8
QK_NOPE = 64
QK_ROPE = 32
V_HEAD = 64
KV_LORA = 128
Q_LORA = 256
MLA_WIDTH = MLA_HEADS * V_HEAD
GLA_HEADS = 4
GLA_DK = 64
GLA_DV = 128
GLA_WIDTH = GLA_HEADS * GLA_DV
GLA_QK = GLA_HEADS * GLA_DK
GLA_GATE_RANK = 16
GLA_TAU = 16.0
P_DIM = 256
NORM_EPS = 1e-6
ROPE_THETA = 10000.0
QK_LAT = KV_LORA + QK_ROPE
LAT_WIDTH = MLA_HEADS * KV_LORA
Z_WIDTH = 2560
GLA_CHUNK = 64
LOG2_E = 1.4426950408889634
GLA_SAMPLE_CHUNK = 16
GLA_SAMPLE_SEQS = 8
GLA_UNROLL = 8
DECODE_CHUNK_KEYS = 4096
MLA_TILES_PER_STEP = 2
MLA_CHUNK_COLS = 1024
ONES_ROWS = 16
DMA_LOOP_UNROLL = 8

LANES = 128
VMEM_LIMIT = 56 * 1024 * 1024


def _rms(x):
    return x * lax.rsqrt(jnp.mean(x * x, axis=-1, keepdims=True) + NORM_EPS)


def _silu(x):
    return x * (1.0 / (1.0 + jnp.exp(-x)))


def _swap16(x):
    n = x.shape[-1]
    lane = lax.broadcasted_iota(jnp.int32, x.shape, x.ndim - 1)
    fwd = pltpu.roll(x, n - 16, axis=x.ndim - 1)
    bwd = pltpu.roll(x, 16, axis=x.ndim - 1)
    return jnp.where((lane & 31) < 16, fwd, bwd)


def _absorb_kernel(wq_nope_ref, wuk_bd_ref, wq_rope_ref, o_ref):
    o_ref[:, :LAT_WIDTH] = jnp.dot(wq_nope_ref[...], wuk_bd_ref[...], preferred_element_type=F32,
                                   precision=lax.Precision.HIGHEST).astype(o_ref.dtype)
    o_ref[:, LAT_WIDTH:] = wq_rope_ref[...].astype(o_ref.dtype)


def _absorb_call(wq_nope, wuk_bd, wq_rope):
    return pl.pallas_call(
        _absorb_kernel,
        out_shape=jax.ShapeDtypeStruct((Q_LORA, LAT_WIDTH + MLA_HEADS * QK_ROPE), BF16),
        compiler_params=pltpu.CompilerParams(vmem_limit_bytes=VMEM_LIMIT),
        name="absorb",
    )(wq_nope, wuk_bd, wq_rope)


def _proj_kernel(x_ref, cos_ref, sin_ref, gmix_ref, wall_ref, gqn_ref, wq_ref,
                 gkvn_ref, wa2_ref, ba_ref,
                 q_out, kv_out, ckv_out, kr_out, gm_out, gqk_out, gv_out, la_out, gg_out, ct_out=None,
                 *, scale, tq):
    x = x_ref[...]
    xn = (_rms(x) * gmix_ref[...]).astype(BF16)
    z = jnp.dot(xn, wall_ref[...], preferred_element_type=F32)
    cos = cos_ref[...]
    sin = sin_ref[...]

    qn = (_rms(z[:, 0:Q_LORA]) * gqn_ref[...]).astype(BF16)
    q = jnp.dot(qn, wq_ref[...], preferred_element_type=F32)
    q_lat = q[:, :LAT_WIDTH] * scale
    qr = q[:, LAT_WIDTH:]
    cos2 = jnp.concatenate([cos, cos], axis=1)
    sin2 = jnp.concatenate([sin, sin], axis=1)
    qr = (qr * cos2 + _swap16(qr) * sin2) * scale
    if tq is None:
        nb, _, lb, _ = q_out.shape
        for h in range(MLA_HEADS):
            q_h = jnp.concatenate([q_lat[:, h * KV_LORA:(h + 1) * KV_LORA],
                                   qr[:, h * QK_ROPE:(h + 1) * QK_ROPE]], axis=1)
            q_out[:, h, :, :] = q_h.reshape(nb, lb, QK_LAT).astype(q_out.dtype)
    else:
        q_lat_t = jnp.transpose(q_lat).astype(q_out.dtype)
        qr_t = jnp.transpose(qr).astype(q_out.dtype)
        for h in range(MLA_HEADS):
            for j in range(q_out.shape[1]):
                tok = slice(j * tq, (j + 1) * tq)
                q_out[0, j, 0:KV_LORA, h * tq:(h + 1) * tq] = q_lat_t[h * KV_LORA:(h + 1) * KV_LORA, tok]
                q_out[0, j, KV_LORA:QK_LAT, h * tq:(h + 1) * tq] = qr_t[h * QK_ROPE:(h + 1) * QK_ROPE, tok]

    ckv = _rms(z[:, 256:384]) * gkvn_ref[...]
    ckv_out[...] = ckv
    blk = z[:, 384:512]
    kr_blk = blk * cos + _swap16(blk) * sin
    kr = kr_blk[:, :QK_ROPE]
    kv_out[:, 0:KV_LORA] = ckv.astype(kv_out.dtype)
    kv_out[:, KV_LORA:QK_LAT] = kr.astype(kv_out.dtype)
    if tq is None:
        kr_out[...] = kr
    else:
        kr_out[0] = jnp.transpose(kr_blk)[0:QK_ROPE, :]
        ct_out[0] = jnp.transpose(ckv).astype(ct_out.dtype)

    a_pre = jnp.dot(blk.astype(BF16), wa2_ref[...], preferred_element_type=F32) + ba_ref[...]
    la_out[...] = (jnp.minimum(a_pre, 0.0) - jnp.log1p(jnp.exp(-jnp.abs(a_pre)))) * (1.0 / GLA_TAU)

    gm_out[...] = _silu(z[:, 512:1024]).astype(gm_out.dtype)
    gqk_out[...] = z[:, 1024:1536]
    gv_out[...] = z[:, 1536:2048].astype(gv_out.dtype)
    gg_out[...] = _silu(z[:, 2048:2560]).astype(gg_out.dtype)


def _proj_call(x2d, cos_t, sin_t, w, *, n_seq, seq_len, ts, q_dtype, table_period, tq=None):
    n_tok = x2d.shape[0]
    n_tiles = n_tok // ts
    tok = lambda width: pl.BlockSpec((ts, width), lambda i: (i, 0))
    full = lambda a: pl.BlockSpec(a.shape, lambda i: (0,) * a.ndim)
    tab = pl.BlockSpec((ts, LANES), lambda i: (i % table_period, 0))
    if tq is None:
        assert ts % seq_len == 0
        q_shape = jax.ShapeDtypeStruct((n_seq, MLA_HEADS, seq_len, QK_LAT), q_dtype)
        q_spec = pl.BlockSpec((ts // seq_len, MLA_HEADS, seq_len, QK_LAT), lambda i: (i, 0, 0, 0))
        kr_shape, kr_spec = jax.ShapeDtypeStruct((n_tok, QK_ROPE), F32), tok(QK_ROPE)
        extra_shape, extra_spec = (), ()
    else:
        assert seq_len % ts == 0 and ts % tq == 0
        per = seq_len // ts
        q_shape = jax.ShapeDtypeStruct((n_seq, seq_len // tq, QK_LAT, MLA_HEADS * tq), q_dtype)
        q_spec = pl.BlockSpec((1, ts // tq, QK_LAT, MLA_HEADS * tq), lambda i: (i // per, i % per, 0, 0))
        kr_shape = jax.ShapeDtypeStruct((n_seq, QK_ROPE, seq_len), F32)
        kr_spec = pl.BlockSpec((1, QK_ROPE, ts), lambda i: (i // per, 0, i % per))
        extra_shape = (jax.ShapeDtypeStruct((n_seq, KV_LORA, seq_len), q_dtype),)
        extra_spec = (pl.BlockSpec((1, KV_LORA, ts), lambda i: (i // per, 0, i % per)),)
    out_shape = (
        q_shape,
        jax.ShapeDtypeStruct((n_tok, QK_LAT), q_dtype),
        jax.ShapeDtypeStruct((n_tok, KV_LORA), F32),
        kr_shape,
        jax.ShapeDtypeStruct((n_tok, MLA_WIDTH), BF16),
        jax.ShapeDtypeStruct((n_tok, 2 * GLA_QK), F32),
        jax.ShapeDtypeStruct((n_tok, GLA_WIDTH), BF16),
        jax.ShapeDtypeStruct((n_tok, GLA_QK), F32),
        jax.ShapeDtypeStruct((n_tok, GLA_WIDTH), BF16),
    ) + extra_shape
    out_specs = (
        q_spec,
        tok(QK_LAT), tok(KV_LORA), kr_spec, tok(MLA_WIDTH), tok(2 * GLA_QK),
        tok(GLA_WIDTH), tok(GLA_QK), tok(GLA_WIDTH),
    ) + extra_spec
    weights = (w["g_mix"], w["w_all"], w["g_qn"], w["w_q"], w["g_kvn"],
               w["w_a2"], w["b_a"])
    return pl.pallas_call(
        functools.partial(_proj_kernel, scale=(QK_NOPE + QK_ROPE) ** -0.5 * LOG2_E, tq=tq),
        grid=(n_tiles,),
        in_specs=[tok(D_MODEL), tab, tab] + [full(a) for a in weights],
        out_specs=out_specs,
        out_shape=out_shape,
        compiler_params=pltpu.CompilerParams(dimension_semantics=("arbitrary",),
                                             vmem_limit_bytes=VMEM_LIMIT),
        name="proj",
    )(x2d, cos_t, sin_t, *weights)


def _mla_prompt_kernel(qi_ref, kj_ref, qt_ref, kv_ref, ct_ref, wuvt_ref, o_ref, m_sc, acc_sc, s_buf, p_buf,
                       *, tq, tk, cw, n_sub):
    step = pl.program_id(1)
    qi = qi_ref[step]
    kg = kj_ref[step]
    last_kj = (qi * tq + (tq - 1)) // tk
    n_cols = MLA_HEADS * tq

    @pl.when(kg == 0)
    def _():
        m_sc[...] = jnp.full(m_sc.shape, -jnp.inf, F32)
        acc_sc[...] = jnp.zeros(acc_sc.shape, F32)

    def process(subs, masked):
        kvs = {sub: kv_ref[0, sub * tk:(sub + 1) * tk, :] for sub in subs}
        cts = {sub: jnp.concatenate([ct_ref[0, :, sub * tk:(sub + 1) * tk], jnp.ones((ONES_ROWS, tk), BF16)],
                                    axis=0) for sub in subs}
        work = [(sub, c) for sub in subs for c in range(n_cols // cw)]

        def scores(i):
            sub, c = work[i]
            cols = slice(c * cw, (c + 1) * cw)
            s = jnp.dot(kvs[sub], qt_ref[0, 0, :, cols], preferred_element_type=F32)
            if masked:
                kpos = (kg * n_sub + sub) * tk + lax.broadcasted_iota(jnp.int32, (tk, cw), 0)
                qpos = qi * tq + ((c * cw + lax.broadcasted_iota(jnp.int32, (tk, cw), 1)) & (tq - 1))
                s = jnp.where(kpos <= qpos, s, -jnp.inf)
            s_buf[i % 3] = s
            m_prev = m_sc[:, cols]
            m_new = jnp.maximum(m_prev, jnp.max(s, axis=0, keepdims=True))
            m_sc[:, cols] = m_new
            return m_new, jnp.exp2(m_prev - m_new)

        def probs(i, m_new):
            p_buf[i % 2] = jnp.exp2((s_buf[i % 3] - m_new).astype(BF16))

        def values(i, alpha):
            sub, c = work[i]
            cols = slice(c * cw, (c + 1) * cw)
            acc_sc[:, cols] = alpha * acc_sc[:, cols] + jnp.dot(cts[sub], p_buf[i % 2],
                                                                preferred_element_type=F32)

        stats = {}
        for t in range(len(work) + 3):
            if t < len(work):
                stats[t] = scores(t)
            if 0 <= t - 2 < len(work):
                probs(t - 2, stats[t - 2][0])
            if 0 <= t - 3 < len(work):
                values(t - 3, stats[t - 3][1])

    first_kj = kg * n_sub

    @pl.when(first_kj + (n_sub - 1) < last_kj)
    def _():
        process(list(range(n_sub)), False)

    @pl.when(first_kj + (n_sub - 1) >= last_kj)
    def _():
        for sub in range(n_sub):
            @pl.when(first_kj + sub < last_kj)
            def _():
                process([sub], False)

            @pl.when(first_kj + sub == last_kj)
            def _():
                process([sub], True)

    @pl.when(kg == last_kj // n_sub)
    def _():
        for pair in range(MLA_HEADS // 2):
            halves = []
            for h in (2 * pair, 2 * pair + 1):
                cols = slice(h * tq, (h + 1) * tq)
                o_lat_t = acc_sc[0:KV_LORA, cols] * (1.0 / acc_sc[KV_LORA:KV_LORA + 1, cols])
                halves.append(jnp.dot(wuvt_ref[h], o_lat_t.astype(BF16), preferred_element_type=F32))
            o_ref[0, :, pair * LANES:(pair + 1) * LANES] = jnp.transpose(
                jnp.concatenate(halves, axis=0)).astype(o_ref.dtype)


def _mla_prompt_call(qt, kv, ct, w_uv_t, *, tq, tk):
    n_b, n_qb, _, n_cols = qt.shape
    seq = n_qb * tq
    assert tq & (tq - 1) == 0
    cw = min(n_cols, MLA_CHUNK_COLS)
    n_sub = MLA_TILES_PER_STEP if seq % (MLA_TILES_PER_STEP * tk) == 0 else 1
    qi_l, kj_l = [], []
    for i in range(n_qb):
        for g in range(((i * tq + tq - 1) // tk) // n_sub + 1):
            qi_l.append(i)
            kj_l.append(g)
    qi_t = jnp.asarray(np.array(qi_l, np.int32))
    kj_t = jnp.asarray(np.array(kj_l, np.int32))
    grid_spec = pltpu.PrefetchScalarGridSpec(
        num_scalar_prefetch=2,
        grid=(n_b, len(qi_l)),
        in_specs=[
            pl.BlockSpec((1, 1, QK_LAT, n_cols), lambda b, s, qi, kj: (b, qi[s], 0, 0)),
            pl.BlockSpec((1, n_sub * tk, QK_LAT), lambda b, s, qi, kj: (b, kj[s], 0)),
            pl.BlockSpec((1, KV_LORA, n_sub * tk), lambda b, s, qi, kj: (b, 0, kj[s])),
            pl.BlockSpec(w_uv_t.shape, lambda b, s, qi, kj: (0, 0, 0)),
        ],
        out_specs=pl.BlockSpec((1, tq, MLA_WIDTH), lambda b, s, qi, kj: (b, qi[s], 0)),
        scratch_shapes=[pltpu.VMEM((1, n_cols), F32), pltpu.VMEM((KV_LORA + ONES_ROWS, n_cols), F32),
                        pltpu.VMEM((3, tk, cw), F32), pltpu.VMEM((2, tk, cw), BF16)],
    )
    return pl.pallas_call(
        functools.partial(_mla_prompt_kernel, tq=tq, tk=tk, cw=cw, n_sub=n_sub),
        grid_spec=grid_spec,
        out_shape=jax.ShapeDtypeStruct((n_b, seq, MLA_WIDTH), BF16),
        compiler_params=pltpu.CompilerParams(dimension_semantics=("arbitrary", "arbitrary"),
                                             vmem_limit_bytes=VMEM_LIMIT),
        name="mla_prompt",
    )(qi_t, kj_t, qt, kv, ct, w_uv_t)


def _mla_decode_kernel(pt_ref, q_ref, kvn_ref, lat_hbm, krt_hbm, o_ref,
                       lat_buf, krt_buf, sems, *, n_pages, page, n_new, kc):
    b = pl.program_id(0)
    slot = b % 2
    rows = MLA_HEADS * n_new

    def page_copies(seq, slot_, i):
        pg = pt_ref[seq * n_pages + i]
        off = i * page if isinstance(i, int) else pl.multiple_of(i * page, page)
        return (pltpu.make_async_copy(lat_hbm.at[pg], lat_buf.at[slot_, pl.ds(off, page), :], sems.at[0, slot_]),
                pltpu.make_async_copy(krt_hbm.at[pg], krt_buf.at[slot_, i], sems.at[1, slot_]))

    def for_pages(seq, slot_, lo, hi, action):
        def body(i, carry):
            for cp in page_copies(seq, slot_, i):
                action(cp)
            return carry
        lax.fori_loop(lo, hi, body, 0, unroll=DMA_LOOP_UNROLL)

    start = lambda cp: cp.start()
    wait = lambda cp: cp.wait()
    last = pl.num_programs(0) - 1

    @pl.when(b == 0)
    def _():
        for_pages(0, 0, 0, n_pages, start)

    for_pages(b, slot, 0, n_pages, wait)
    nxt = jnp.minimum(b + 1, last)
    n_kc = n_pages * page // kc

    q = q_ref[0].reshape(rows, QK_LAT).astype(BF16)
    q_lat = q[:, :KV_LORA]
    q_rope = q[:, KV_LORA:]
    contract_last = (((1,), (1,)), ((), ()))
    lats, scores = [], []
    for c in range(n_kc):
        for i in range(c * n_pages // n_kc, (c + 1) * n_pages // n_kc):
            for cp in page_copies(nxt, 1 - slot, i):
                cp.start()
        lat = lat_buf[slot, c * kc:(c + 1) * kc, :].astype(BF16)
        krt = jnp.concatenate([krt_buf[slot, i] for i in range(c * kc // page, (c + 1) * kc // page)],
                              axis=1).astype(BF16)
        lats.append(lat)
        scores.append(lax.dot_general(q_lat, lat, contract_last, preferred_element_type=F32)
                      + jnp.dot(q_rope, krt, preferred_element_type=F32))
    kn = jnp.concatenate([kvn_ref[0], jnp.zeros((LANES - n_new, QK_LAT), F32)], axis=0).astype(BF16)
    sn = lax.dot_general(q, kn, contract_last, preferred_element_type=F32)
    tpos = lax.broadcasted_iota(jnp.int32, (rows, LANES), 0) & (n_new - 1)
    jpos = lax.broadcasted_iota(jnp.int32, (rows, LANES), 1)
    lats.append(kn[:, :KV_LORA])
    scores.append(jnp.where(jpos <= tpos, sn, -jnp.inf))

    m = functools.reduce(jnp.maximum, [jnp.max(s, axis=1, keepdims=True) for s in scores])
    probs = [jnp.exp2(s - m) for s in scores]
    l = functools.reduce(jnp.add, [jnp.sum(p, axis=1, keepdims=True) for p in probs])
    acc = functools.reduce(jnp.add, [jnp.dot(p.astype(BF16), v, preferred_element_type=F32)
                                     for p, v in zip(probs, lats)])
    o = acc * (1.0 / l)
    for h in range(MLA_HEADS):
        o_ref[0, :, h * KV_LORA:(h + 1) * KV_LORA] = o[h * n_new:(h + 1) * n_new, :].astype(o_ref.dtype)

    @pl.when(b == last)
    def _():
        for_pages(nxt, 1 - slot, 0, n_pages, wait)


def _mla_decode_call(q, kv_new, lat_pool, krt_pool, page_table):
    n_b, _, n_new, _ = q.shape
    assert n_new & (n_new - 1) == 0 and n_new <= LANES
    n_pages = page_table.shape[1]
    page = lat_pool.shape[1]
    past = n_pages * page
    kc = _pick_tile(past, DECODE_CHUNK_KEYS)
    grid_spec = pltpu.PrefetchScalarGridSpec(
        num_scalar_prefetch=1,
        grid=(n_b,),
        in_specs=[pl.BlockSpec((1, MLA_HEADS, n_new, QK_LAT), lambda b, pt: (b, 0, 0, 0)),
                  pl.BlockSpec((1, n_new, QK_LAT), lambda b, pt: (b, 0, 0)),
                  pl.BlockSpec(memory_space=pl.ANY),
                  pl.BlockSpec(memory_space=pl.ANY)],
        out_specs=pl.BlockSpec((1, n_new, LAT_WIDTH), lambda b, pt: (b, 0, 0)),
        scratch_shapes=[pltpu.VMEM((2, past, KV_LORA), F32), pltpu.VMEM((2, n_pages, QK_ROPE, page), F32),
                        pltpu.SemaphoreType.DMA((2, 2))],
    )
    return pl.pallas_call(
        functools.partial(_mla_decode_kernel, n_pages=n_pages, page=page, n_new=n_new, kc=kc),
        grid_spec=grid_spec,
        out_shape=jax.ShapeDtypeStruct((n_b, n_new, LAT_WIDTH), BF16),
        compiler_params=pltpu.CompilerParams(dimension_semantics=("arbitrary",),
                                             vmem_limit_bytes=VMEM_LIMIT),
        name="mla_decode",
    )(page_table.reshape(-1), q, kv_new, lat_pool, krt_pool)


def _gla_levels(chunk):
    levels = []
    h = chunk // 2
    while h >= 1:
        levels.append(h)
        h //= 2
    return levels


def _gla_decay_matrix(chunk):
    t = np.arange(chunk)[:, None]
    u = np.arange(chunk)[None, :]
    blocks = [(u <= t), (u > t)]
    for h in _gla_levels(chunk):
        off = t % (2 * h)
        mid = t - off + h
        upper = (off >= h) & (u >= mid) & (u <= t)
        lower = (off < h) & (u > t) & (u < mid)
        blocks.append(upper | lower)
    return np.concatenate(blocks, axis=0).astype(np.float32)


def _gla_chunk(qk, v, la, g_mat, gn, s_in, s_out, o_ref, r0, *, chunk):
    levels = _gla_levels(chunk)
    la_hi = la.astype(BF16)
    la_lo = (la - la_hi.astype(F32)).astype(BF16)
    e_all = jnp.dot(g_mat, jnp.concatenate([la_hi, la_lo], axis=0),
                    preferred_element_type=F32)
    q = qk[:, :GLA_QK] * (GLA_DK ** -0.5)
    k = qk[:, GLA_QK:]
    e_cum = e_all[0:chunk]
    q_in = q * jnp.exp(e_cum)
    k_out = k * jnp.exp(e_all[chunk:2 * chunk])
    row = lax.broadcasted_iota(jnp.int32, (chunk, GLA_QK), 0)
    q_lv, k_lv = [q], [k]
    for li, h in enumerate(levels):
        w = jnp.exp(e_all[(2 + li) * chunk:(3 + li) * chunk])
        upper = (row & (2 * h - 1)) >= h
        q_lv.append(jnp.where(upper, q * w, 0.0))
        k_lv.append(jnp.where(upper, 0.0, k * w))
    contract_last = (((1,), (1,)), ((), ()))
    contract_first = (((0,), (0,)), ((), ()))
    head_of_lane = lax.broadcasted_iota(jnp.int32, (chunk, GLA_QK), 1) // GLA_DK
    stack_heads = lambda x: jnp.concatenate(
        [jnp.where(head_of_lane == hd, x, 0.0) for hd in range(GLA_HEADS)], axis=0).astype(BF16)
    trow = lax.broadcasted_iota(jnp.int32, (GLA_HEADS * chunk, chunk), 0) & (chunk - 1)
    tcol = lax.broadcasted_iota(jnp.int32, (GLA_HEADS * chunk, chunk), 1)
    masks = [trow == tcol] + [(trow // (2 * h)) == (tcol // (2 * h)) for h in levels]
    a_all = jnp.zeros((GLA_HEADS * chunk, chunk), F32)
    for ql, kl, mk in zip(q_lv, k_lv, masks):
        a_all = a_all + jnp.where(mk, lax.dot_general(stack_heads(ql), kl.astype(BF16), contract_last,
                                                      preferred_element_type=F32), 0.0)
    a_all = a_all.astype(BF16)
    lane = lax.broadcasted_iota(jnp.int32, (chunk, LANES), 1)
    decay_row = e_cum[chunk - 1:chunk, :]
    for p in range(GLA_HEADS // 2):
        ls = slice(p * LANES, (p + 1) * LANES)
        s_pair = s_in[ls, :]
        decay = jnp.exp(jnp.transpose(jnp.broadcast_to(decay_row[:, ls], (LANES, LANES))))
        k_out_pair = k_out[:, ls].astype(BF16)
        new_rows = []
        for half in range(2):
            hd = 2 * p + half
            in_head = (lane // GLA_DK) == half
            v_h = v[:, hd * GLA_DV:(hd + 1) * GLA_DV]
            qi = jnp.where(in_head, q_in[:, ls], 0.0).astype(BF16)
            lhs = jnp.concatenate([qi, a_all[hd * chunk:(hd + 1) * chunk]], axis=1)
            rhs = jnp.concatenate([s_pair.astype(BF16), v_h], axis=0)
            o = jnp.dot(lhs, rhs, preferred_element_type=F32)
            o_ref[pl.ds(r0, chunk), hd * GLA_DV:(hd + 1) * GLA_DV] = (_rms(o) * gn).astype(o_ref.dtype)
            upd = lax.dot_general(k_out_pair, v_h, contract_first, preferred_element_type=F32)
            hr = slice(half * GLA_DK, (half + 1) * GLA_DK)
            new_rows.append(decay[hr, :] * s_pair[hr, :] + upd[hr, :])
        s_out[ls, :] = jnp.concatenate(new_rows, axis=0)


def _gla_kernel(*refs, chunk, n_sub, independent):
    if independent:
        qk_ref, v_ref, la_ref, g_ref, gn_ref, s0_ref, o_ref, st_ref = refs
    else:
        qk_ref, v_ref, la_ref, g_ref, gn_ref, o_ref, st_ref, s_sc = refs
        j = pl.program_id(1)

        @pl.when(j == 0)
        def _():
            s_sc[...] = jnp.zeros(s_sc.shape, F32)

    g_mat = g_ref[...]
    gn = gn_ref[...]

    def body(c, carry):
        r0 = pl.multiple_of(c * chunk, chunk)
        s_in, s_out = (s0_ref.at[c], st_ref.at[c]) if independent else (s_sc, s_sc)
        _gla_chunk(qk_ref[pl.ds(r0, chunk), :], v_ref[pl.ds(r0, chunk), :], la_ref[pl.ds(r0, chunk), :],
                   g_mat, gn, s_in, s_out, o_ref, r0, chunk=chunk)
        return carry

    lax.fori_loop(0, n_sub, body, 0, unroll=min(n_sub, GLA_UNROLL))

    if not independent:
        @pl.when(j == pl.num_programs(1) - 1)
        def _():
            st_ref[0] = s_sc[...]


def _gla_call(qk, v, la, gn, s0, *, n_seq, seq_len, rows, chunk):
    independent = s0 is not None
    g_np = _gla_decay_matrix(chunk)
    g_mat = jnp.asarray(np.concatenate([g_np, g_np], axis=1), BF16)
    full = lambda a: pl.BlockSpec(a.shape, lambda b, j: (0,) * a.ndim)
    if independent:
        assert seq_len == chunk and (n_seq * chunk) % rows == 0
        grid = (n_seq * chunk // rows, 1)
        tok = lambda width: pl.BlockSpec((rows, width), lambda b, j: (b, 0))
        state = pl.BlockSpec((rows // chunk, GLA_QK, GLA_DV), lambda b, j: (b, 0, 0))
        args, extra_specs, scratch = [qk, v, la, g_mat, gn, s0], [state], []
    else:
        per = seq_len // rows
        grid = (n_seq, per)
        tok = lambda width: pl.BlockSpec((rows, width), lambda b, j: (b * per + j, 0))
        state = pl.BlockSpec((1, GLA_QK, GLA_DV), lambda b, j: (b, 0, 0))
        args, extra_specs, scratch = [qk, v, la, g_mat, gn], [], [pltpu.VMEM((GLA_QK, GLA_DV), F32)]
    return pl.pallas_call(
        functools.partial(_gla_kernel, chunk=chunk, n_sub=rows // chunk, independent=independent),
        grid=grid,
        in_specs=[tok(2 * GLA_QK), tok(GLA_WIDTH), tok(GLA_QK), full(g_mat), full(gn)] + extra_specs,
        out_specs=(tok(GLA_WIDTH), state),
        out_shape=(jax.ShapeDtypeStruct((n_seq * seq_len, GLA_WIDTH), BF16),
                   jax.ShapeDtypeStruct((n_seq, GLA_QK, GLA_DV), F32)),
        scratch_shapes=scratch,
        compiler_params=pltpu.CompilerParams(dimension_semantics=("arbitrary", "arbitrary"),
                                             vmem_limit_bytes=VMEM_LIMIT),
        name="gla",
    )(*args)


def _out_kernel(x_ref, omla_ref, gm_ref, ogla_ref, gg_ref, p_ref, wuv_ref, wout_ref, wpg_ref, wpp_ref,
                gfin_ref, y_ref, *, latent_in):
    if latent_in:
        o_mla = jnp.dot(omla_ref[...], wuv_ref[...], preferred_element_type=F32)
    else:
        o_mla = omla_ref[...].astype(F32)
    o_mla = o_mla * gm_ref[...].astype(F32)
    o_gla = ogla_ref[...].astype(F32) * gg_ref[...].astype(F32)
    mix = jnp.concatenate([o_mla, o_gla], axis=1).astype(BF16)
    h = x_ref[...] + jnp.dot(mix, wout_ref[...], preferred_element_type=F32)
    gate = jnp.dot(h.astype(BF16), wpg_ref[...], preferred_element_type=F32)
    gate = 1.0 / (1.0 + jnp.exp(-gate))
    emb = jnp.dot(p_ref[...].astype(BF16), wpp_ref[...], preferred_element_type=F32)
    h = h + gate * emb
    y_ref[...] = _rms(h) * gfin_ref[...]


def _out_call(x2d, o_mla, gm, o_gla, gg, p2d, w, *, ts):
    n_tok = x2d.shape[0]
    mla_w = o_mla.shape[1]
    tok = lambda width: pl.BlockSpec((ts, width), lambda i: (i, 0))
    full = lambda a: pl.BlockSpec(a.shape, lambda i: (0,) * a.ndim)
    weights = (w["w_uv_bd"], w["w_out"], w["w_pg"], w["w_pp"], w["g_final"])
    return pl.pallas_call(
        functools.partial(_out_kernel, latent_in=(mla_w == LAT_WIDTH)),
        grid=(n_tok // ts,),
        in_specs=[tok(D_MODEL), tok(mla_w), tok(MLA_WIDTH), tok(GLA_WIDTH), tok(GLA_WIDTH), tok(P_DIM)]
        + [full(a) for a in weights],
        out_specs=tok(D_MODEL),
        out_shape=jax.ShapeDtypeStruct((n_tok, D_MODEL), F32),
        compiler_params=pltpu.CompilerParams(dimension_semantics=("arbitrary",),
                                             vmem_limit_bytes=VMEM_LIMIT),
        name="out",
    )(x2d, o_mla, gm, o_gla, gg, p2d, *weights)


def _prep_weights(g_mix_norm, w_in, g_qnorm, w_qup, g_kvnorm, w_uk, w_uv, w_gla_a2, b_gla_a, g_gla_onorm,
                  w_out, w_ple_gate, w_ple_proj, g_final):
    o_kv = Q_LORA
    o_gm = o_kv + KV_LORA + QK_ROPE
    o_gq = o_gm + MLA_WIDTH
    o_gk = o_gq + GLA_QK
    o_gv = o_gk + GLA_QK
    o_a = o_gv + GLA_WIDTH
    o_gg = o_a + GLA_GATE_RANK
    pad = jnp.zeros((D_MODEL, 512 - (Q_LORA + KV_LORA + QK_ROPE + GLA_GATE_RANK)), F32)
    w_all = jnp.concatenate([
        w_in[:, 0:o_gm], w_in[:, o_a:o_gg], pad,
        w_in[:, o_gm:o_gq],
        w_in[:, o_gq:o_gv],
        w_in[:, o_gv:o_a],
        w_in[:, o_gg:],
    ], axis=1).astype(BF16)
    assert w_all.shape == (D_MODEL, Z_WIDTH)
    w_qup3 = w_qup.reshape(Q_LORA, MLA_HEADS, QK_NOPE + QK_ROPE)
    eye = jnp.eye(MLA_HEADS, dtype=F32)
    w_uk_bd = (eye[:, None, :, None] * jnp.transpose(w_uk, (1, 2, 0))[:, :, None, :]
               ).reshape(MLA_HEADS * QK_NOPE, LAT_WIDTH)
    w_q = _absorb_call(w_qup3[:, :, :QK_NOPE].reshape(Q_LORA, MLA_HEADS * QK_NOPE), w_uk_bd,
                       w_qup3[:, :, QK_NOPE:].reshape(Q_LORA, MLA_HEADS * QK_ROPE))
    w_uv_bd =(eye[:, None, :, None] * jnp.transpose(w_uv, (1, 0, 2))[:, :, None, :]
               ).reshape(LAT_WIDTH, MLA_WIDTH).astype(BF16)
    w_a2 = jnp.zeros((LANES, GLA_QK), F32).at[QK_ROPE:QK_ROPE + GLA_GATE_RANK].set(w_gla_a2).astype(BF16)
    row = lambda a: a.reshape(1, -1).astype(F32)
    return dict(g_mix=row(g_mix_norm), w_all=w_all, g_qn=row(g_qnorm), w_q=w_q,
                g_kvn=row(g_kvnorm), w_a2=w_a2, b_a=row(b_gla_a), g_gla=row(g_gla_onorm),
                w_uv_bd=w_uv_bd, w_uv_t=jnp.transpose(w_uv, (1, 2, 0)).astype(BF16), w_out=w_out.astype(BF16), w_pg=w_ple_gate.astype(BF16),
                w_pp=w_ple_proj.astype(BF16), g_final=row(g_final))


def _rope_tables(pos):
    inv = 1.0 / (ROPE_THETA ** (jnp.arange(0, QK_ROPE, 2, dtype=F32) / QK_ROPE))
    ang = pos[:, None] * inv[None, :]
    cos, sin = jnp.cos(ang), jnp.sin(ang)
    reps = LANES // QK_ROPE
    return (jnp.tile(jnp.concatenate([cos, cos], axis=1), (1, reps)),
            jnp.tile(jnp.concatenate([-sin, sin], axis=1), (1, reps)))


def _pick_tile(n, pref):
    t = min(n, pref)
    while n % t:
        t //= 2
    return t


def kernel(x_prompt, x_sample, p_prompt, p_sample, cache_ckv, cache_krope, state_gla, page_table, g_mix_norm, w_in, g_qnorm, w_qup, g_kvnorm, w_uk, w_uv, w_gla_a2, b_gla_a, g_gla_onorm, w_out, w_ple_gate, w_ple_proj, g_final):
    n_b, seq, _ = x_prompt.shape
    n_d, n_new, _ = x_sample.shape
    assert w_in.shape[0] == 1, "single-layer stack"
    page = cache_ckv.shape[2]
    past_len = page_table.shape[1] * page
    w = _prep_weights(g_mix_norm[0], w_in[0], g_qnorm[0], w_qup[0], g_kvnorm[0], w_uk[0], w_uv[0],
                      w_gla_a2[0], b_gla_a[0], g_gla_onorm[0], w_out[0], w_ple_gate[0], w_ple_proj[0], g_final)

    ts_p = _pick_tile(seq, 512)
    cos_p, sin_p = _rope_tables(jnp.arange(seq, dtype=F32))
    xp = x_prompt.reshape(n_b * seq, D_MODEL)
    tq = _pick_tile(seq, 512)
    tk = _pick_tile(seq, 512)
    (qt_p, kv_p, ckv_p, kr_p, gm_p, gqk_p, gv_p, la_p, gg_p, ct_p) = _proj_call(
        xp, cos_p, sin_p, w, n_seq=n_b, seq_len=seq, ts=ts_p, q_dtype=BF16, table_period=seq // ts_p, tq=tq)
    omla_p = _mla_prompt_call(qt_p, kv_p.reshape(n_b, seq, QK_LAT), ct_p, w["w_uv_t"], tq=tq, tk=tk)
    chunk_p = min(GLA_CHUNK, seq)
    ogla_p, st_p = _gla_call(gqk_p, gv_p, la_p, w["g_gla"], None, n_seq=n_b, seq_len=seq,
                             rows=_pick_tile(seq, 1024), chunk=chunk_p)
    y_p = _out_call(xp, omla_p.reshape(n_b * seq, MLA_WIDTH), gm_p, ogla_p, gg_p,
                    p_prompt[0].reshape(n_b * seq, P_DIM), w, ts=ts_p)

    n_tok_s = n_d * n_new
    ts_s = _pick_tile(n_tok_s, 512)
    cos_s, sin_s = _rope_tables(past_len + jnp.arange(n_new, dtype=F32))
    cos_s = jnp.tile(cos_s, (ts_s // n_new, 1))
    sin_s = jnp.tile(sin_s, (ts_s // n_new, 1))
    xs = x_sample.reshape(n_tok_s, D_MODEL)
    (q_s, kv_s, ckv_s, kr_s, gm_s, gqk_s, gv_s, la_s, gg_s) = _proj_call(
        xs, cos_s, sin_s, w, n_seq=n_d, seq_len=n_new, ts=ts_s, q_dtype=F32, table_period=1)
    olat_s = _mla_decode_call(q_s, kv_s.reshape(n_d, n_new, QK_LAT), cache_ckv[0],
                              jnp.swapaxes(cache_krope[0], 1, 2), page_table)
    chunk_s = max(n_new, GLA_SAMPLE_CHUNK)
    pad_tok = lambda a: jnp.pad(a.reshape(n_d, n_new, -1), ((0, 0), (0, chunk_s - n_new), (0, 0))
                                ).reshape(n_d * chunk_s, -1)
    ogla_s, st_s = _gla_call(pad_tok(gqk_s), pad_tok(gv_s), pad_tok(la_s), w["g_gla"],
                             state_gla[0].reshape(n_d, GLA_QK, GLA_DV),
                             n_seq=n_d, seq_len=chunk_s, rows=chunk_s * _pick_tile(n_d, GLA_SAMPLE_SEQS),
                             chunk=chunk_s)
    ogla_s = ogla_s.reshape(n_d, chunk_s, GLA_WIDTH)[:, :n_new].reshape(n_tok_s, GLA_WIDTH)
    y_s = _out_call(xs, olat_s.reshape(n_tok_s, LAT_WIDTH), gm_s, ogla_s, gg_s,
                    p_sample[0].reshape(n_tok_s, P_DIM), w, ts=ts_s)

    return (y_p.reshape(n_b, seq, D_MODEL),
            y_s.reshape(n_d, n_new, D_MODEL),
            ckv_p.reshape(1, n_b, seq, KV_LORA),
            jnp.swapaxes(kr_p, 1, 2).reshape(1, n_b, seq, QK_ROPE),
            st_p.reshape(1, n_b, GLA_HEADS, GLA_DK, GLA_DV),
            ckv_s.reshape(1, n_d, n_new, KV_LORA),
            kr_s.reshape(1, n_d, n_new, QK_ROPE),
            st_s.reshape(1, n_d, GLA_HEADS, GLA_DK, GLA_DV))
```

```python
import functools

import numpy as np
import jax
import jax.numpy as jnp
from jax import lax
from jax.experimental import pallas as pl
from jax.experimental.pallas import tpu as pltpu

F32 = jnp.float32
BF16 = jnp.bfloat16

D_MODEL = 1024
MLA_HEADS = 8
QK_NOPE = 64
QK_ROPE = 32
V_HEAD = 64
KV_LORA = 128
Q_LORA = 256
MLA_WIDTH = MLA_HEADS * V_HEAD
GLA_HEADS = 4
GLA_DK = 64
GLA_DV = 128
GLA_WIDTH = GLA_HEADS * GLA_DV
GLA_QK = GLA_HEADS * GLA_DK
GLA_GATE_RANK = 16
GLA_TAU = 16.0
P_DIM = 256
NORM_EPS = 1e-6
ROPE_THETA = 10000.0
QK_LAT = KV_LORA + QK_ROPE
LAT_WIDTH = MLA_HEADS * KV_LORA
LANES = 128
Z_CKV = Q_LORA
Z_KR = Z_CKV + KV_LORA
Z_GM = Z_KR + LANES
Z_GQK = Z_GM + MLA_WIDTH
Z_GV = Z_GQK + 2 * GLA_QK
Z_GG = Z_GV + GLA_WIDTH
Z_WIDTH = Z_GG + GLA_WIDTH
GLA_CHUNK = 64
LOG2_E = 1.4426950408889634
GLA_SAMPLE_CHUNK = 16
GLA_SAMPLE_SEQS = 8
GLA_UNROLL = 8
DECODE_CHUNK_KEYS = 4096
MLA_TILES_PER_STEP = 2
MLA_CHUNK_COLS = 1024
ONES_ROWS = 16
DMA_LOOP_UNROLL = 8

VMEM_LIMIT = 56 * 1024 * 1024


def _rms(x):
    return x * lax.rsqrt(jnp.mean(x * x, axis=-1, keepdims=True) + NORM_EPS)


def _silu(x):
    return x * (1.0 / (1.0 + jnp.exp(-x)))


def _swap16(x):
    n = x.shape[-1]
    half = QK_ROPE // 2
    lane = lax.broadcasted_iota(jnp.int32, x.shape, x.ndim - 1)
    fwd = pltpu.roll(x, n - half, axis=x.ndim - 1)
    bwd = pltpu.roll(x, half, axis=x.ndim - 1)
    return jnp.where((lane & (QK_ROPE - 1)) < half, fwd, bwd)


def _absorb_kernel(wq_nope_ref, wuk_bd_ref, wq_rope_ref, o_ref):
    o_ref[:, :LAT_WIDTH] = jnp.dot(wq_nope_ref[...], wuk_bd_ref[...], preferred_element_type=F32,
                                   precision=lax.Precision.HIGHEST).astype(o_ref.dtype)
    o_ref[:, LAT_WIDTH:] = wq_rope_ref[...].astype(o_ref.dtype)


def _absorb_call(wq_nope, wuk_bd, wq_rope):
    return pl.pallas_call(
        _absorb_kernel,
        out_shape=jax.ShapeDtypeStruct((Q_LORA, LAT_WIDTH + MLA_HEADS * QK_ROPE), BF16),
        compiler_params=pltpu.CompilerParams(vmem_limit_bytes=VMEM_LIMIT),
        name="absorb",
    )(wq_nope, wuk_bd, wq_rope)


def _proj_kernel(x_ref, cos_ref, sin_ref, gmix_ref, wall_ref, gqn_ref, wq_ref,
                 gkvn_ref, wa2_ref, ba_ref,
                 q_out, kv_out, ckv_out, kr_out, gm_out, gqk_out, gv_out, la_out, gg_out, ct_out=None,
                 *, scale, tq):
    x = x_ref[...]
    xn = (_rms(x) * gmix_ref[...]).astype(BF16)
    z = jnp.dot(xn, wall_ref[...], preferred_element_type=F32)
    cos = cos_ref[...]
    sin = sin_ref[...]

    qn = (_rms(z[:, 0:Q_LORA]) * gqn_ref[...]).astype(BF16)
    q = jnp.dot(qn, wq_ref[...], preferred_element_type=F32)
    q_lat = q[:, :LAT_WIDTH] * scale
    qr = q[:, LAT_WIDTH:]
    cos2 = jnp.concatenate([cos, cos], axis=1)
    sin2 = jnp.concatenate([sin, sin], axis=1)
    qr = (qr * cos2 + _swap16(qr) * sin2) * scale
    if tq is None:
        nb, _, lb, _ = q_out.shape
        for h in range(MLA_HEADS):
            q_h = jnp.concatenate([q_lat[:, h * KV_LORA:(h + 1) * KV_LORA],
                                   qr[:, h * QK_ROPE:(h + 1) * QK_ROPE]], axis=1)
            q_out[:, h, :, :] = q_h.reshape(nb, lb, QK_LAT).astype(q_out.dtype)
    else:
        q_lat_t = jnp.transpose(q_lat).astype(q_out.dtype)
        qr_t = jnp.transpose(qr).astype(q_out.dtype)
        for h in range(MLA_HEADS):
            for j in range(q_out.shape[1]):
                tok = slice(j * tq, (j + 1) * tq)
                q_out[0, j, 0:KV_LORA, h * tq:(h + 1) * tq] = q_lat_t[h * KV_LORA:(h + 1) * KV_LORA, tok]
                q_out[0, j, KV_LORA:QK_LAT, h * tq:(h + 1) * tq] = qr_t[h * QK_ROPE:(h + 1) * QK_ROPE, tok]

    ckv = _rms(z[:, Z_CKV:Z_KR]) * gkvn_ref[...]
    ckv_out[...] = ckv
    blk = z[:, Z_KR:Z_GM]
    kr_blk = blk * cos + _swap16(blk) * sin
    kr = kr_blk[:, :QK_ROPE]
    kv_out[:, 0:KV_LORA] = ckv.astype(kv_out.dtype)
    kv_out[:, KV_LORA:QK_LAT] = kr.astype(kv_out.dtype)
    if tq is None:
        kr_out[...] = kr
    else:
        kr_out[0] = jnp.transpose(kr_blk)[0:QK_ROPE, :]
        ct_out[0] = jnp.transpose(ckv).astype(ct_out.dtype)

    a_pre = jnp.dot(blk.astype(BF16), wa2_ref[...], preferred_element_type=F32) + ba_ref[...]
    la_out[...] = (jnp.minimum(a_pre, 0.0) - jnp.log1p(jnp.exp(-jnp.abs(a_pre)))) * (1.0 / GLA_TAU)

    gm_out[...] = _silu(z[:, Z_GM:Z_GQK]).astype(gm_out.dtype)
    gqk_out[...] = z[:, Z_GQK:Z_GV]
    gv_out[...] = z[:, Z_GV:Z_GG].astype(gv_out.dtype)
    gg_out[...] = _silu(z[:, Z_GG:Z_WIDTH]).astype(gg_out.dtype)


def _proj_call(x2d, cos_t, sin_t, w, *, n_seq, seq_len, ts, q_dtype, table_period, tq=None):
    n_tok = x2d.shape[0]
    n_tiles = n_tok // ts
    tok = lambda width: pl.BlockSpec((ts, width), lambda i: (i, 0))
    full = lambda a: pl.BlockSpec(a.shape, lambda i: (0,) * a.ndim)
    tab = pl.BlockSpec((ts, LANES), lambda i: (i % table_period, 0))
    if tq is None:
        assert ts % seq_len == 0
        q_shape = jax.ShapeDtypeStruct((n_seq, MLA_HEADS, seq_len, QK_LAT), q_dtype)
        q_spec = pl.BlockSpec((ts // seq_len, MLA_HEADS, seq_len, QK_LAT), lambda i: (i, 0, 0, 0))
        kr_shape, kr_spec = jax.ShapeDtypeStruct((n_tok, QK_ROPE), F32), tok(QK_ROPE)
        extra_shape, extra_spec = (), ()
    else:
        assert seq_len % ts == 0 and ts % tq == 0
        per = seq_len // ts
        q_shape = jax.ShapeDtypeStruct((n_seq, seq_len // tq, QK_LAT, MLA_HEADS * tq), q_dtype)
        q_spec = pl.BlockSpec((1, ts // tq, QK_LAT, MLA_HEADS * tq), lambda i: (i // per, i % per, 0, 0))
        kr_shape = jax.ShapeDtypeStruct((n_seq, QK_ROPE, seq_len), F32)
        kr_spec = pl.BlockSpec((1, QK_ROPE, ts), lambda i: (i // per, 0, i % per))
        extra_shape = (jax.ShapeDtypeStruct((n_seq, KV_LORA, seq_len), q_dtype),)
        extra_spec = (pl.BlockSpec((1, KV_LORA, ts), lambda i: (i // per, 0, i % per)),)
    out_shape = (
        q_shape,
        jax.ShapeDtypeStruct((n_tok, QK_LAT), q_dtype),
        jax.ShapeDtypeStruct((n_tok, KV_LORA), F32),
        kr_shape,
        jax.ShapeDtypeStruct((n_tok, MLA_WIDTH), BF16),
        jax.ShapeDtypeStruct((n_tok, 2 * GLA_QK), F32),
        jax.ShapeDtypeStruct((n_tok, GLA_WIDTH), BF16),
        jax.ShapeDtypeStruct((n_tok, GLA_QK), F32),
        jax.ShapeDtypeStruct((n_tok, GLA_WIDTH), BF16),
    ) + extra_shape
    out_specs = (
        q_spec,
        tok(QK_LAT), tok(KV_LORA), kr_spec, tok(MLA_WIDTH), tok(2 * GLA_QK),
        tok(GLA_WIDTH), tok(GLA_QK), tok(GLA_WIDTH),
    ) + extra_spec
    weights = (w["g_mix"], w["w_all"], w["g_qn"], w["w_q"], w["g_kvn"],
               w["w_a2"], w["b_a"])
    return pl.pallas_call(
        functools.partial(_proj_kernel, scale=(QK_NOPE + QK_ROPE) ** -0.5 * LOG2_E, tq=tq),
        grid=(n_tiles,),
        in_specs=[tok(D_MODEL), tab, tab] + [full(a) for a in weights],
        out_specs=out_specs,
        out_shape=out_shape,
        compiler_params=pltpu.CompilerParams(dimension_semantics=("arbitrary",),
                                             vmem_limit_bytes=VMEM_LIMIT),
        name="proj",
    )(x2d, cos_t, sin_t, *weights)


def _mla_prompt_kernel(qi_ref, kj_ref, qt_ref, kv_ref, ct_ref, wuvt_ref, o_ref, m_sc, acc_sc, s_buf, p_buf,
                       *, tq, tk, cw, n_sub):
    step = pl.program_id(1)
    qi = qi_ref[step]
    kg = kj_ref[step]
    last_kj = (qi * tq + (tq - 1)) // tk
    n_cols = MLA_HEADS * tq

    @pl.when(kg == 0)
    def _():
        m_sc[...] = jnp.full(m_sc.shape, -jnp.inf, F32)
        acc_sc[...] = jnp.zeros(acc_sc.shape, F32)

    def process(subs, masked_sub=None):
        kvs = {sub: kv_ref[0, sub * tk:(sub + 1) * tk, :] for sub in subs}
        cts = {sub: jnp.concatenate([ct_ref[0, :, sub * tk:(sub + 1) * tk], jnp.ones((ONES_ROWS, tk), BF16)],
                                    axis=0) for sub in subs}
        work = [(sub, c) for sub in subs for c in range(n_cols // cw)]

        def scores(i):
            sub, c = work[i]
            cols = slice(c * cw, (c + 1) * cw)
            s = jnp.dot(kvs[sub], qt_ref[0, 0, :, cols], preferred_element_type=F32)
            if sub == masked_sub:
                kpos = (kg * n_sub + sub) * tk + lax.broadcasted_iota(jnp.int32, (tk, cw), 0)
                qpos = qi * tq + ((c * cw + lax.broadcasted_iota(jnp.int32, (tk, cw), 1)) & (tq - 1))
                s = jnp.where(kpos <= qpos, s, -jnp.inf)
            s_buf[i % 3] = s
            m_prev = m_sc[:, cols]
            m_new = jnp.maximum(m_prev, jnp.max(s, axis=0, keepdims=True))
            m_sc[:, cols] = m_new
            return m_new, jnp.exp2(m_prev - m_new)

        def probs(i, m_new):
            p_buf[i % 2] = jnp.exp2((s_buf[i % 3] - m_new).astype(BF16))

        def values(i, alpha):
            sub, c = work[i]
            cols = slice(c * cw, (c + 1) * cw)
            acc_sc[:, cols] = alpha * acc_sc[:, cols] + jnp.dot(cts[sub], p_buf[i % 2],
                                                                preferred_element_type=F32)

        stats = {}
        for t in range(len(work) + 3):
            if t < len(work):
                stats[t] = scores(t)
            if 0 <= t - 2 < len(work):
                probs(t - 2, stats[t - 2][0])
            if 0 <= t - 3 < len(work):
                values(t - 3, stats[t - 3][1])

    first_kj = kg * n_sub

    @pl.when(first_kj + (n_sub - 1) < last_kj)
    def _():
        process(list(range(n_sub)))

    for diag in range(n_sub):
        @pl.when(first_kj + diag == last_kj)
        def _():
            process(list(range(diag + 1)), masked_sub=diag)

    @pl.when(kg == last_kj // n_sub)
    def _():
        for pair in range(MLA_HEADS // 2):
            halves = []
            for h in (2 * pair, 2 * pair + 1):
                cols = slice(h * tq, (h + 1) * tq)
                o_lat_t = acc_sc[0:KV_LORA, cols] * (1.0 / acc_sc[KV_LORA:KV_LORA + 1, cols])
                halves.append(jnp.dot(wuvt_ref[h], o_lat_t.astype(BF16), preferred_element_type=F32))
            o_ref[0, :, pair * LANES:(pair + 1) * LANES] = jnp.transpose(
                jnp.concatenate(halves, axis=0)).astype(o_ref.dtype)


def _mla_prompt_call(qt, kv, ct, w_uv_t, *, tq, tk):
    n_b, n_qb, _, n_cols = qt.shape
    seq = n_qb * tq
    assert tq & (tq - 1) == 0
    cw = min(n_cols, MLA_CHUNK_COLS)
    n_sub = MLA_TILES_PER_STEP if seq % (MLA_TILES_PER_STEP * tk) == 0 else 1
    qi_l, kj_l = [], []
    for i in range(n_qb):
        for g in range(((i * tq + tq - 1) // tk) // n_sub + 1):
            qi_l.append(i)
            kj_l.append(g)
    qi_t = jnp.asarray(np.array(qi_l, np.int32))
    kj_t = jnp.asarray(np.array(kj_l, np.int32))
    grid_spec = pltpu.PrefetchScalarGridSpec(
        num_scalar_prefetch=2,
        grid=(n_b, len(qi_l)),
        in_specs=[
            pl.BlockSpec((1, 1, QK_LAT, n_cols), lambda b, s, qi, kj: (b, qi[s], 0, 0)),
            pl.BlockSpec((1, n_sub * tk, QK_LAT), lambda b, s, qi, kj: (b, kj[s], 0)),
            pl.BlockSpec((1, KV_LORA, n_sub * tk), lambda b, s, qi, kj: (b, 0, kj[s])),
            pl.BlockSpec(w_uv_t.shape, lambda b, s, qi, kj: (0, 0, 0)),
        ],
        out_specs=pl.BlockSpec((1, tq, MLA_WIDTH), lambda b, s, qi, kj: (b, qi[s], 0)),
        scratch_shapes=[pltpu.VMEM((1, n_cols), F32), pltpu.VMEM((KV_LORA + ONES_ROWS, n_cols), F32),
                        pltpu.VMEM((3, tk, cw), F32), pltpu.VMEM((2, tk, cw), BF16)],
    )
    return pl.pallas_call(
        functools.partial(_mla_prompt_kernel, tq=tq, tk=tk, cw=cw, n_sub=n_sub),
        grid_spec=grid_spec,
        out_shape=jax.ShapeDtypeStruct((n_b, seq, MLA_WIDTH), BF16),
        compiler_params=pltpu.CompilerParams(dimension_semantics=("arbitrary", "arbitrary"),
                                             vmem_limit_bytes=VMEM_LIMIT),
        name="mla_prompt",
    )(qi_t, kj_t, qt, kv, ct, w_uv_t)


def _mla_decode_kernel(pt_ref, q_ref, kvn_ref, lat_hbm, krt_hbm, o_ref,
                       lat_buf, krt_buf, sems, *, n_pages, page, n_new, kc):
    b = pl.program_id(0)
    slot = b % 2
    rows = MLA_HEADS * n_new

    def page_copies(seq, slot_, i):
        pg = pt_ref[seq * n_pages + i]
        off = i * page if isinstance(i, int) else pl.multiple_of(i * page, page)
        return (pltpu.make_async_copy(lat_hbm.at[pg], lat_buf.at[slot_, pl.ds(off, page), :], sems.at[0, slot_]),
                pltpu.make_async_copy(krt_hbm.at[pg], krt_buf.at[slot_, i], sems.at[1, slot_]))

    def for_pages(seq, slot_, lo, hi, action):
        def body(i, carry):
            for cp in page_copies(seq, slot_, i):
                action(cp)
            return carry
        lax.fori_loop(lo, hi, body, 0, unroll=DMA_LOOP_UNROLL)

    start = lambda cp: cp.start()
    wait = lambda cp: cp.wait()
    last = pl.num_programs(0) - 1

    @pl.when(b == 0)
    def _():
        for_pages(0, 0, 0, n_pages, start)

    for_pages(b, slot, 0, n_pages, wait)
    nxt = jnp.minimum(b + 1, last)
    n_kc = n_pages * page // kc

    q = q_ref[0].reshape(rows, QK_LAT).astype(BF16)
    q_lat = q[:, :KV_LORA]
    q_rope = q[:, KV_LORA:]
    contract_last = (((1,), (1,)), ((), ()))
    lats, scores = [], []
    for c in range(n_kc):
        for i in range(c * n_pages // n_kc, (c + 1) * n_pages // n_kc):
            for cp in page_copies(nxt, 1 - slot, i):
                cp.start()
        lat = lat_buf[slot, c * kc:(c + 1) * kc, :].astype(BF16)
        krt = jnp.concatenate([krt_buf[slot, i] for i in range(c * kc // page, (c + 1) * kc // page)],
                              axis=1).astype(BF16)
        lats.append(lat)
        scores.append(lax.dot_general(q_lat, lat, contract_last, preferred_element_type=F32)
                      + jnp.dot(q_rope, krt, preferred_element_type=F32))
    kn = jnp.concatenate([kvn_ref[0], jnp.zeros((LANES - n_new, QK_LAT), F32)], axis=0).astype(BF16)
    sn = lax.dot_general(q, kn, contract_last, preferred_element_type=F32)
    tpos = lax.broadcasted_iota(jnp.int32, (rows, LANES), 0) & (n_new - 1)
    jpos = lax.broadcasted_iota(jnp.int32, (rows, LANES), 1)
    lats.append(kn[:, :KV_LORA])
    scores.append(jnp.where(jpos <= tpos, sn, -jnp.inf))

    m = functools.reduce(jnp.maximum, [jnp.max(s, axis=1, keepdims=True) for s in scores])
    probs = [jnp.exp2(s - m) for s in scores]
    l = functools.reduce(jnp.add, [jnp.sum(p, axis=1, keepdims=True) for p in probs])
    acc = functools.reduce(jnp.add, [jnp.dot(p.astype(BF16), v, preferred_element_type=F32)
                                     for p, v in zip(probs, lats)])
    o = acc * (1.0 / l)
    for h in range(MLA_HEADS):
        o_ref[0, :, h * KV_LORA:(h + 1) * KV_LORA] = o[h * n_new:(h + 1) * n_new, :].astype(o_ref.dtype)

    @pl.when(b == last)
    def _():
        for_pages(nxt, 1 - slot, 0, n_pages, wait)


def _mla_decode_call(q, kv_new, lat_pool, krt_pool, page_table):
    n_b, _, n_new, _ = q.shape
    assert n_new & (n_new - 1) == 0 and n_new <= LANES
    n_pages = page_table.shape[1]
    page = lat_pool.shape[1]
    past = n_pages * page
    kc = _pick_tile(past, DECODE_CHUNK_KEYS)
    grid_spec = pltpu.PrefetchScalarGridSpec(
        num_scalar_prefetch=1,
        grid=(n_b,),
        in_specs=[pl.BlockSpec((1, MLA_HEADS, n_new, QK_LAT), lambda b, pt: (b, 0, 0, 0)),
                  pl.BlockSpec((1, n_new, QK_LAT), lambda b, pt: (b, 0, 0)),
                  pl.BlockSpec(memory_space=pl.ANY),
                  pl.BlockSpec(memory_space=pl.ANY)],
        out_specs=pl.BlockSpec((1, n_new, LAT_WIDTH), lambda b, pt: (b, 0, 0)),
        scratch_shapes=[pltpu.VMEM((2, past, KV_LORA), F32), pltpu.VMEM((2, n_pages, QK_ROPE, page), F32),
                        pltpu.SemaphoreType.DMA((2, 2))],
    )
    return pl.pallas_call(
        functools.partial(_mla_decode_kernel, n_pages=n_pages, page=page, n_new=n_new, kc=kc),
        grid_spec=grid_spec,
        out_shape=jax.ShapeDtypeStruct((n_b, n_new, LAT_WIDTH), BF16),
        compiler_params=pltpu.CompilerParams(dimension_semantics=("arbitrary",),
                                             vmem_limit_bytes=VMEM_LIMIT),
        name="mla_decode",
    )(page_table.reshape(-1), q, kv_new, lat_pool, krt_pool)


def _gla_levels(chunk):
    levels = []
    h = chunk // 2
    while h >= 1:
        levels.append(h)
        h //= 2
    return levels


def _gla_decay_matrix(chunk):
    t = np.arange(chunk)[:, None]
    u = np.arange(chunk)[None, :]
    blocks = [(u <= t), (u > t)]
    for h in _gla_levels(chunk):
        off = t % (2 * h)
        mid = t - off + h
        upper = (off >= h) & (u >= mid) & (u <= t)
        lower = (off < h) & (u > t) & (u < mid)
        blocks.append(upper | lower)
    return np.concatenate(blocks, axis=0).astype(np.float32)


def _gla_chunk(qk, v, la, g_mat, gn, s_in, s_out, o_ref, r0, *, chunk):
    levels = _gla_levels(chunk)
    la_hi = la.astype(BF16)
    la_lo = (la - la_hi.astype(F32)).astype(BF16)
    e_all = jnp.dot(g_mat, jnp.concatenate([la_hi, la_lo], axis=0),
                    preferred_element_type=F32)
    q = qk[:, :GLA_QK] * (GLA_DK ** -0.5)
    k = qk[:, GLA_QK:]
    e_cum = e_all[0:chunk]
    q_in = q * jnp.exp(e_cum)
    k_out = k * jnp.exp(e_all[chunk:2 * chunk])
    row = lax.broadcasted_iota(jnp.int32, (chunk, GLA_QK), 0)
    q_lv, k_lv = [q], [k]
    for li, h in enumerate(levels):
        w = jnp.exp(e_all[(2 + li) * chunk:(3 + li) * chunk])
        upper = (row & (2 * h - 1)) >= h
        q_lv.append(jnp.where(upper, q * w, 0.0))
        k_lv.append(jnp.where(upper, 0.0, k * w))
    contract_last = (((1,), (1,)), ((), ()))
    contract_first = (((0,), (0,)), ((), ()))
    head_of_lane = lax.broadcasted_iota(jnp.int32, (chunk, GLA_QK), 1) // GLA_DK
    stack_heads = lambda x: jnp.concatenate(
        [jnp.where(head_of_lane == hd, x, 0.0) for hd in range(GLA_HEADS)], axis=0).astype(BF16)
    trow = lax.broadcasted_iota(jnp.int32, (GLA_HEADS * chunk, chunk), 0) & (chunk - 1)
    tcol = lax.broadcasted_iota(jnp.int32, (GLA_HEADS * chunk, chunk), 1)
    masks = [trow == tcol] + [(trow // (2 * h)) == (tcol // (2 * h)) for h in levels]
    a_all = jnp.zeros((GLA_HEADS * chunk, chunk), F32)
    for ql, kl, mk in zip(q_lv, k_lv, masks):
        a_all = a_all + jnp.where(mk, lax.dot_general(stack_heads(ql), kl.astype(BF16), contract_last,
                                                      preferred_element_type=F32), 0.0)
    a_all = a_all.astype(BF16)
    lane = lax.broadcasted_iota(jnp.int32, (chunk, LANES), 1)
    decay_row = e_cum[chunk - 1:chunk, :]
    for p in range(GLA_HEADS // 2):
        ls = slice(p * LANES, (p + 1) * LANES)
        s_pair = s_in[ls, :]
        decay = jnp.exp(jnp.transpose(jnp.broadcast_to(decay_row[:, ls], (LANES, LANES))))
        k_out_pair = k_out[:, ls].astype(BF16)
        new_rows = []
        for half in range(2):
            hd = 2 * p + half
            in_head = (lane // GLA_DK) == half
            v_h = v[:, hd * GLA_DV:(hd + 1) * GLA_DV]
            qi = jnp.where(in_head, q_in[:, ls], 0.0).astype(BF16)
            lhs = jnp.concatenate([qi, a_all[hd * chunk:(hd + 1) * chunk]], axis=1)
            rhs = jnp.concatenate([s_pair.astype(BF16), v_h], axis=0)
            o = jnp.dot(lhs, rhs, preferred_element_type=F32)
            o_ref[pl.ds(r0, chunk), hd * GLA_DV:(hd + 1) * GLA_DV] = (_rms(o) * gn).astype(o_ref.dtype)
            upd = lax.dot_general(k_out_pair, v_h, contract_first, preferred_element_type=F32)
            hr = slice(half * GLA_DK, (half + 1) * GLA_DK)
            new_rows.append(decay[hr, :] * s_pair[hr, :] + upd[hr, :])
        s_out[ls, :] = jnp.concatenate(new_rows, axis=0)


def _gla_kernel(*refs, chunk, n_sub, independent):
    if independent:
        qk_ref, v_ref, la_ref, g_ref, gn_ref, s0_ref, o_ref, st_ref = refs
    else:
        qk_ref, v_ref, la_ref, g_ref, gn_ref, o_ref, st_ref, s_sc = refs
        j = pl.program_id(1)

        @pl.when(j == 0)
        def _():
            s_sc[...] = jnp.zeros(s_sc.shape, F32)

    g_mat = g_ref[...]
    gn = gn_ref[...]

    def body(c, carry):
        r0 = pl.multiple_of(c * chunk, chunk)
        s_in, s_out = (s0_ref.at[c], st_ref.at[c]) if independent else (s_sc, s_sc)
        _gla_chunk(qk_ref[pl.ds(r0, chunk), :], v_ref[pl.ds(r0, chunk), :], la_ref[pl.ds(r0, chunk), :],
                   g_mat, gn, s_in, s_out, o_ref, r0, chunk=chunk)
        return carry

    lax.fori_loop(0, n_sub, body, 0, unroll=min(n_sub, GLA_UNROLL))

    if not independent:
        @pl.when(j == pl.num_programs(1) - 1)
        def _():
            st_ref[0] = s_sc[...]


def _gla_call(qk, v, la, gn, s0, *, n_seq, seq_len, rows, chunk):
    independent = s0 is not None
    g_np = _gla_decay_matrix(chunk)
    g_mat = jnp.asarray(np.concatenate([g_np, g_np], axis=1), BF16)
    full = lambda a: pl.BlockSpec(a.shape, lambda b, j: (0,) * a.ndim)
    if independent:
        assert seq_len == chunk and (n_seq * chunk) % rows == 0
        grid = (n_seq * chunk // rows, 1)
        tok = lambda width: pl.BlockSpec((rows, width), lambda b, j: (b, 0))
        state = pl.BlockSpec((rows // chunk, GLA_QK, GLA_DV), lambda b, j: (b, 0, 0))
        args, extra_specs, scratch = [qk, v, la, g_mat, gn, s0], [state], []
    else:
        per = seq_len // rows
        grid = (n_seq, per)
        tok = lambda width: pl.BlockSpec((rows, width), lambda b, j: (b * per + j, 0))
        state = pl.BlockSpec((1, GLA_QK, GLA_DV), lambda b, j: (b, 0, 0))
        args, extra_specs, scratch = [qk, v, la, g_mat, gn], [], [pltpu.VMEM((GLA_QK, GLA_DV), F32)]
    return pl.pallas_call(
        functools.partial(_gla_kernel, chunk=chunk, n_sub=rows // chunk, independent=independent),
        grid=grid,
        in_specs=[tok(2 * GLA_QK), tok(GLA_WIDTH), tok(GLA_QK), full(g_mat), full(gn)] + extra_specs,
        out_specs=(tok(GLA_WIDTH), state),
        out_shape=(jax.ShapeDtypeStruct((n_seq * seq_len, GLA_WIDTH), BF16),
                   jax.ShapeDtypeStruct((n_seq, GLA_QK, GLA_DV), F32)),
        scratch_shapes=scratch,
        compiler_params=pltpu.CompilerParams(dimension_semantics=("arbitrary", "arbitrary"),
                                             vmem_limit_bytes=VMEM_LIMIT),
        name="gla",
    )(*args)


def _out_kernel(x_ref, omla_ref, gm_ref, ogla_ref, gg_ref, p_ref, wuv_ref, wout_ref, wpg_ref, wpp_ref,
                gfin_ref, y_ref, *, latent_in):
    if latent_in:
        o_mla = jnp.dot(omla_ref[...], wuv_ref[...], preferred_element_type=F32)
    else:
        o_mla = omla_ref[...].astype(F32)
    o_mla = o_mla * gm_ref[...].astype(F32)
    o_gla = ogla_ref[...].astype(F32) * gg_ref[...].astype(F32)
    mix = jnp.concatenate([o_mla, o_gla], axis=1).astype(BF16)
    h = x_ref[...] + jnp.dot(mix, wout_ref[...], preferred_element_type=F32)
    gate = jnp.dot(h.astype(BF16), wpg_ref[...], preferred_element_type=F32)
    gate = 1.0 / (1.0 + jnp.exp(-gate))
    emb = jnp.dot(p_ref[...].astype(BF16), wpp_ref[...], preferred_element_type=F32)
    h = h + gate * emb
    y_ref[...] = _rms(h) * gfin_ref[...]


def _out_call(x2d, o_mla, gm, o_gla, gg, p2d, w, *, ts):
    n_tok = x2d.shape[0]
    mla_w = o_mla.shape[1]
    tok = lambda width: pl.BlockSpec((ts, width), lambda i: (i, 0))
    full = lambda a: pl.BlockSpec(a.shape, lambda i: (0,) * a.ndim)
    weights = (w["w_uv_bd"], w["w_out"], w["w_pg"], w["w_pp"], w["g_final"])
    return pl.pallas_call(
        functools.partial(_out_kernel, latent_in=(mla_w == LAT_WIDTH)),
        grid=(n_tok // ts,),
        in_specs=[tok(D_MODEL), tok(mla_w), tok(MLA_WIDTH), tok(GLA_WIDTH), tok(GLA_WIDTH), tok(P_DIM)]
        + [full(a) for a in weights],
        out_specs=tok(D_MODEL),
        out_shape=jax.ShapeDtypeStruct((n_tok, D_MODEL), F32),
        compiler_params=pltpu.CompilerParams(dimension_semantics=("arbitrary",),
                                             vmem_limit_bytes=VMEM_LIMIT),
        name="out",
    )(x2d, o_mla, gm, o_gla, gg, p2d, *weights)


def _prep_weights(g_mix_norm, w_in, g_qnorm, w_qup, g_kvnorm, w_uk, w_uv, w_gla_a2, b_gla_a, g_gla_onorm,
                  w_out, w_ple_gate, w_ple_proj, g_final):
    o_kv = Q_LORA
    o_gm = o_kv + KV_LORA + QK_ROPE
    o_gq = o_gm + MLA_WIDTH
    o_gk = o_gq + GLA_QK
    o_gv = o_gk + GLA_QK
    o_a = o_gv + GLA_WIDTH
    o_gg = o_a + GLA_GATE_RANK
    pad = jnp.zeros((D_MODEL, LANES - QK_ROPE - GLA_GATE_RANK), F32)
    w_all = jnp.concatenate([
        w_in[:, 0:o_gm], w_in[:, o_a:o_gg], pad,
        w_in[:, o_gm:o_gq],
        w_in[:, o_gq:o_gv],
        w_in[:, o_gv:o_a],
        w_in[:, o_gg:],
    ], axis=1).astype(BF16)
    assert w_all.shape == (D_MODEL, Z_WIDTH)
    w_qup3 = w_qup.reshape(Q_LORA, MLA_HEADS, QK_NOPE + QK_ROPE)
    eye = jnp.eye(MLA_HEADS, dtype=F32)
    w_uk_bd = (eye[:, None, :, None] * jnp.transpose(w_uk, (1, 2, 0))[:, :, None, :]
               ).reshape(MLA_HEADS * QK_NOPE, LAT_WIDTH)
    w_q = _absorb_call(w_qup3[:, :, :QK_NOPE].reshape(Q_LORA, MLA_HEADS * QK_NOPE), w_uk_bd,
                       w_qup3[:, :, QK_NOPE:].reshape(Q_LORA, MLA_HEADS * QK_ROPE))
    w_uv_bd =(eye[:, None, :, None] * jnp.transpose(w_uv, (1, 0, 2))[:, :, None, :]
               ).reshape(LAT_WIDTH, MLA_WIDTH).astype(BF16)
    w_a2 = jnp.zeros((LANES, GLA_QK), F32).at[QK_ROPE:QK_ROPE + GLA_GATE_RANK].set(w_gla_a2).astype(BF16)
    row = lambda a: a.reshape(1, -1).astype(F32)
    return dict(g_mix=row(g_mix_norm), w_all=w_all, g_qn=row(g_qnorm), w_q=w_q,
                g_kvn=row(g_kvnorm), w_a2=w_a2, b_a=row(b_gla_a), g_gla=row(g_gla_onorm),
                w_uv_bd=w_uv_bd, w_uv_t=jnp.transpose(w_uv, (1, 2, 0)).astype(BF16), w_out=w_out.astype(BF16), w_pg=w_ple_gate.astype(BF16),
                w_pp=w_ple_proj.astype(BF16), g_final=row(g_final))


def _rope_tables(pos):
    inv = 1.0 / (ROPE_THETA ** (jnp.arange(0, QK_ROPE, 2, dtype=F32) / QK_ROPE))
    ang = pos[:, None] * inv[None, :]
    cos, sin = jnp.cos(ang), jnp.sin(ang)
    reps = LANES // QK_ROPE
    return (jnp.tile(jnp.concatenate([cos, cos], axis=1), (1, reps)),
            jnp.tile(jnp.concatenate([-sin, sin], axis=1), (1, reps)))


def _pick_tile(n, pref):
    t = min(n, pref)
    while n % t:
        t //= 2
    return t


def kernel(x_prompt, x_sample, p_prompt, p_sample, cache_ckv, cache_krope, state_gla, page_table, g_mix_norm, w_in, g_qnorm, w_qup, g_kvnorm, w_uk, w_uv, w_gla_a2, b_gla_a, g_gla_onorm, w_out, w_ple_gate, w_ple_proj, g_final):
    n_b, seq, _ = x_prompt.shape
    n_d, n_new, _ = x_sample.shape
    assert w_in.shape[0] == 1, "single-layer stack"
    page = cache_ckv.shape[2]
    past_len = page_table.shape[1] * page
    w = _prep_weights(g_mix_norm[0], w_in[0], g_qnorm[0], w_qup[0], g_kvnorm[0], w_uk[0], w_uv[0],
                      w_gla_a2[0], b_gla_a[0], g_gla_onorm[0], w_out[0], w_ple_gate[0], w_ple_proj[0], g_final)

    ts_p = _pick_tile(seq, 512)
    cos_p, sin_p = _rope_tables(jnp.arange(seq, dtype=F32))
    xp = x_prompt.reshape(n_b * seq, D_MODEL)
    tq = _pick_tile(seq, 512)
    tk = _pick_tile(seq, 512)
    (qt_p, kv_p, ckv_p, kr_p, gm_p, gqk_p, gv_p, la_p, gg_p, ct_p) = _proj_call(
        xp, cos_p, sin_p, w, n_seq=n_b, seq_len=seq, ts=ts_p, q_dtype=BF16, table_period=seq // ts_p, tq=tq)
    omla_p = _mla_prompt_call(qt_p, kv_p.reshape(n_b, seq, QK_LAT), ct_p, w["w_uv_t"], tq=tq, tk=tk)
    chunk_p = min(GLA_CHUNK, seq)
    ogla_p, st_p = _gla_call(gqk_p, gv_p, la_p, w["g_gla"], None, n_seq=n_b, seq_len=seq,
                             rows=_pick_tile(seq, 1024), chunk=chunk_p)
    y_p = _out_call(xp, omla_p.reshape(n_b * seq, MLA_WIDTH), gm_p, ogla_p, gg_p,
                    p_prompt[0].reshape(n_b * seq, P_DIM), w, ts=ts_p)

    n_tok_s = n_d * n_new
    ts_s = _pick_tile(n_tok_s, 512)
    cos_s, sin_s = _rope_tables(past_len + jnp.arange(n_new, dtype=F32))
    cos_s = jnp.tile(cos_s, (ts_s // n_new, 1))
    sin_s = jnp.tile(sin_s, (ts_s // n_new, 1))
    xs = x_sample.reshape(n_tok_s, D_MODEL)
    (q_s, kv_s, ckv_s, kr_s, gm_s, gqk_s, gv_s, la_s, gg_s) = _proj_call(
        xs, cos_s, sin_s, w, n_seq=n_d, seq_len=n_new, ts=ts_s, q_dtype=F32, table_period=1)
    olat_s = _mla_decode_call(q_s, kv_s.reshape(n_d, n_new, QK_LAT), cache_ckv[0],
                              jnp.swapaxes(cache_krope[0], 1, 2), page_table)
    chunk_s = max(n_new, GLA_SAMPLE_CHUNK)
    pad_tok = lambda a: jnp.pad(a.reshape(n_d, n_new, -1), ((0, 0), (0, chunk_s - n_new), (0, 0))
                                ).reshape(n_d * chunk_s, -1)
    ogla_s, st_s = _gla_call(pad_tok(gqk_s), pad_tok(gv_s), pad_tok(la_s), w["g_gla"],
                             state_gla[0].reshape(n_d, GLA_QK, GLA_DV),
                             n_seq=n_d, seq_len=chunk_s, rows=chunk_s * _pick_tile(n_d, GLA_SAMPLE_SEQS),
                             chunk=chunk_s)
    ogla_s = ogla_s.reshape(n_d, chunk_s, GLA_WIDTH)[:, :n_new].reshape(n_tok_s, GLA_WIDTH)
    y_s = _out_call(xs, olat_s.reshape(n_tok_s, LAT_WIDTH), gm_s, ogla_s, gg_s,
                    p_sample[0].reshape(n_tok_s, P_DIM), w, ts=ts_s)

    return (y_p.reshape(n_b, seq, D_MODEL),
            y_s.reshape(n_d, n_new, D_MODEL),
            ckv_p.reshape(1, n_b, seq, KV_LORA),
            jnp.swapaxes(kr_p, 1, 2).reshape(1, n_b, seq, QK_ROPE),
            st_p.reshape(1, n_b, GLA_HEADS, GLA_DK, GLA_DV),
            ckv_s.reshape(1, n_d, n_new, KV_LORA),
            kr_s.reshape(1, n_d, n_new, QK_ROPE),
            st_s.reshape(1, n_d, GLA_HEADS, GLA_DK, GLA_DV))
```

```python
import functools

import numpy as np
import jax
import jax.numpy as jnp
from jax import lax
from jax.experimental import pallas as pl
from jax.experimental.pallas import tpu as pltpu

F32 = jnp.float32
BF16 = jnp.bfloat16

D_MODEL = 1024
MLA_HEADS = 8
QK_NOPE = 64
QK_ROPE = 32
V_HEAD = 64
KV_LORA = 128
Q_LORA = 256
MLA_WIDTH = MLA_HEADS * V_HEAD
GLA_HEADS = 4
GLA_DK = 64
GLA_DV = 128
GLA_WIDTH = GLA_HEADS * GLA_DV
GLA_QK = GLA_HEADS * GLA_DK
GLA_GATE_RANK = 16
GLA_TAU = 16.0
P_DIM = 256
NORM_EPS = 1e-6
ROPE_THETA = 10000.0
QK_LAT = KV_LORA + QK_ROPE
LAT_WIDTH = MLA_HEADS * KV_LORA
LANES = 128
Z_CKV = Q_LORA
Z_KR = Z_CKV + KV_LORA
Z_GM = Z_KR + LANES
Z_GQK = Z_GM + MLA_WIDTH
Z_GV = Z_GQK + 2 * GLA_QK
Z_GG = Z_GV + GLA_WIDTH
Z_WIDTH = Z_GG + GLA_WIDTH
GLA_CHUNK = 64
LOG2_E = 1.4426950408889634
GLA_SAMPLE_CHUNK = 16
GLA_SAMPLE_SEQS = 8
GLA_UNROLL = 8
DECODE_CHUNK_KEYS = 4096
MLA_TILES_PER_STEP = 4
MLA_CHUNK_COLS = 1024
ONES_ROWS = 16
DMA_LOOP_UNROLL = 8

VMEM_LIMIT = 56 * 1024 * 1024


def _rms(x):
    return x * lax.rsqrt(jnp.mean(x * x, axis=-1, keepdims=True) + NORM_EPS)


def _silu(x):
    return x * (1.0 / (1.0 + jnp.exp(-x)))


def _swap16(x):
    n = x.shape[-1]
    half = QK_ROPE // 2
    lane = lax.broadcasted_iota(jnp.int32, x.shape, x.ndim - 1)
    fwd = pltpu.roll(x, n - half, axis=x.ndim - 1)
    bwd = pltpu.roll(x, half, axis=x.ndim - 1)
    return jnp.where((lane & (QK_ROPE - 1)) < half, fwd, bwd)


def _absorb_kernel(wq_nope_ref, wuk_bd_ref, wq_rope_ref, o_ref):
    o_ref[:, :LAT_WIDTH] = jnp.dot(wq_nope_ref[...], wuk_bd_ref[...], preferred_element_type=F32,
                                   precision=lax.Precision.HIGHEST).astype(o_ref.dtype)
    o_ref[:, LAT_WIDTH:] = wq_rope_ref[...].astype(o_ref.dtype)


def _absorb_call(wq_nope, wuk_bd, wq_rope):
    return pl.pallas_call(
        _absorb_kernel,
        out_shape=jax.ShapeDtypeStruct((Q_LORA, LAT_WIDTH + MLA_HEADS * QK_ROPE), BF16),
        compiler_params=pltpu.CompilerParams(vmem_limit_bytes=VMEM_LIMIT),
        name="absorb",
    )(wq_nope, wuk_bd, wq_rope)


def _proj_kernel(x_ref, cos_ref, sin_ref, gmix_ref, wall_ref, gqn_ref, wq_ref,
                 gkvn_ref, wa2_ref, ba_ref,
                 q_out, kv_out, ckv_out, kr_out, gm_out, gqk_out, gv_out, la_out, gg_out, ct_out=None,
                 *, scale, tq):
    x = x_ref[...]
    xn = (_rms(x) * gmix_ref[...]).astype(BF16)
    z = jnp.dot(xn, wall_ref[...], preferred_element_type=F32)
    cos = cos_ref[...]
    sin = sin_ref[...]

    qn = (_rms(z[:, 0:Q_LORA]) * gqn_ref[...]).astype(BF16)
    q = jnp.dot(qn, wq_ref[...], preferred_element_type=F32)
    q_lat = q[:, :LAT_WIDTH] * scale
    qr = q[:, LAT_WIDTH:]
    cos2 = jnp.concatenate([cos, cos], axis=1)
    sin2 = jnp.concatenate([sin, sin], axis=1)
    qr = (qr * cos2 + _swap16(qr) * sin2) * scale
    if tq is None:
        nb, _, lb, _ = q_out.shape
        for h in range(MLA_HEADS):
            q_h = jnp.concatenate([q_lat[:, h * KV_LORA:(h + 1) * KV_LORA],
                                   qr[:, h * QK_ROPE:(h + 1) * QK_ROPE]], axis=1)
            q_out[:, h, :, :] = q_h.reshape(nb, lb, QK_LAT).astype(q_out.dtype)
    else:
        q_lat_t = jnp.transpose(q_lat).astype(q_out.dtype)
        qr_t = jnp.transpose(qr).astype(q_out.dtype)
        for h in range(MLA_HEADS):
            for j in range(q_out.shape[1]):
                tok = slice(j * tq, (j + 1) * tq)
                q_out[0, j, 0:KV_LORA, h * tq:(h + 1) * tq] = q_lat_t[h * KV_LORA:(h + 1) * KV_LORA, tok]
                q_out[0, j, KV_LORA:QK_LAT, h * tq:(h + 1) * tq] = qr_t[h * QK_ROPE:(h + 1) * QK_ROPE, tok]

    ckv = _rms(z[:, Z_CKV:Z_KR]) * gkvn_ref[...]
    ckv_out[...] = ckv
    blk = z[:, Z_KR:Z_GM]
    kr_blk = blk * cos + _swap16(blk) * sin
    kr = kr_blk[:, :QK_ROPE]
    kv_out[:, 0:KV_LORA] = ckv.astype(kv_out.dtype)
    kv_out[:, KV_LORA:QK_LAT] = kr.astype(kv_out.dtype)
    if tq is None:
        kr_out[...] = kr
    else:
        kr_out[0] = jnp.transpose(kr_blk)[0:QK_ROPE, :]
        ct_out[0] = jnp.transpose(ckv).astype(ct_out.dtype)

    a_pre = jnp.dot(blk.astype(BF16), wa2_ref[...], preferred_element_type=F32) + ba_ref[...]
    la_out[...] = (jnp.minimum(a_pre, 0.0) - jnp.log1p(jnp.exp(-jnp.abs(a_pre)))) * (1.0 / GLA_TAU)

    gm_out[...] = _silu(z[:, Z_GM:Z_GQK]).astype(gm_out.dtype)
    gqk_out[...] = z[:, Z_GQK:Z_GV]
    gv_out[...] = z[:, Z_GV:Z_GG].astype(gv_out.dtype)
    gg_out[...] = _silu(z[:, Z_GG:Z_WIDTH]).astype(gg_out.dtype)


def _proj_call(x2d, cos_t, sin_t, w, *, n_seq, seq_len, ts, q_dtype, table_period, tq=None):
    n_tok = x2d.shape[0]
    n_tiles = n_tok // ts
    tok = lambda width: pl.BlockSpec((ts, width), lambda i: (i, 0))
    full = lambda a: pl.BlockSpec(a.shape, lambda i: (0,) * a.ndim)
    tab = pl.BlockSpec((ts, LANES), lambda i: (i % table_period, 0))
    if tq is None:
        assert ts % seq_len == 0
        q_shape = jax.ShapeDtypeStruct((n_seq, MLA_HEADS, seq_len, QK_LAT), q_dtype)
        q_spec = pl.BlockSpec((ts // seq_len, MLA_HEADS, seq_len, QK_LAT), lambda i: (i, 0, 0, 0))
        kr_shape, kr_spec = jax.ShapeDtypeStruct((n_tok, QK_ROPE), F32), tok(QK_ROPE)
        extra_shape, extra_spec = (), ()
    else:
        assert seq_len % ts == 0 and ts % tq == 0
        per = seq_len // ts
        q_shape = jax.ShapeDtypeStruct((n_seq, seq_len // tq, QK_LAT, MLA_HEADS * tq), q_dtype)
        q_spec = pl.BlockSpec((1, ts // tq, QK_LAT, MLA_HEADS * tq), lambda i: (i // per, i % per, 0, 0))
        kr_shape = jax.ShapeDtypeStruct((n_seq, QK_ROPE, seq_len), F32)
        kr_spec = pl.BlockSpec((1, QK_ROPE, ts), lambda i: (i // per, 0, i % per))
        extra_shape = (jax.ShapeDtypeStruct((n_seq, KV_LORA, seq_len), q_dtype),)
        extra_spec = (pl.BlockSpec((1, KV_LORA, ts), lambda i: (i // per, 0, i % per)),)
    out_shape = (
        q_shape,
        jax.ShapeDtypeStruct((n_tok, QK_LAT), q_dtype),
        jax.ShapeDtypeStruct((n_tok, KV_LORA), F32),
        kr_shape,
        jax.ShapeDtypeStruct((n_tok, MLA_WIDTH), BF16),
        jax.ShapeDtypeStruct((n_tok, 2 * GLA_QK), F32),
        jax.ShapeDtypeStruct((n_tok, GLA_WIDTH), BF16),
        jax.ShapeDtypeStruct((n_tok, GLA_QK), F32),
        jax.ShapeDtypeStruct((n_tok, GLA_WIDTH), BF16),
    ) + extra_shape
    out_specs = (
        q_spec,
        tok(QK_LAT), tok(KV_LORA), kr_spec, tok(MLA_WIDTH), tok(2 * GLA_QK),
        tok(GLA_WIDTH), tok(GLA_QK), tok(GLA_WIDTH),
    ) + extra_spec
    weights = (w["g_mix"], w["w_all"], w["g_qn"], w["w_q"], w["g_kvn"],
               w["w_a2"], w["b_a"])
    return pl.pallas_call(
        functools.partial(_proj_kernel, scale=(QK_NOPE + QK_ROPE) ** -0.5 * LOG2_E, tq=tq),
        grid=(n_tiles,),
        in_specs=[tok(D_MODEL), tab, tab] + [full(a) for a in weights],
        out_specs=out_specs,
        out_shape=out_shape,
        compiler_params=pltpu.CompilerParams(dimension_semantics=("arbitrary",),
                                             vmem_limit_bytes=VMEM_LIMIT),
        name="proj",
    )(x2d, cos_t, sin_t, *weights)


def _mla_prompt_kernel(qi_ref, kj_ref, qt_ref, kv_ref, ct_ref, wuvt_ref, o_ref, m_sc, acc_sc, s_buf, p_buf,
                       *, tq, tk, cw, n_sub):
    step = pl.program_id(1)
    qi = qi_ref[step]
    kg = kj_ref[step]
    last_kj = (qi * tq + (tq - 1)) // tk
    n_cols = MLA_HEADS * tq

    @pl.when(kg == 0)
    def _():
        m_sc[...] = jnp.full(m_sc.shape, -jnp.inf, F32)
        acc_sc[...] = jnp.zeros(acc_sc.shape, F32)

    def process(subs, masked_sub=None):
        kvs = {sub: kv_ref[0, sub * tk:(sub + 1) * tk, :] for sub in subs}
        cts = {sub: jnp.concatenate([ct_ref[0, :, sub * tk:(sub + 1) * tk], jnp.ones((ONES_ROWS, tk), BF16)],
                                    axis=0) for sub in subs}
        work = [(sub, c) for sub in subs for c in range(n_cols // cw)]

        def scores(i):
            sub, c = work[i]
            cols = slice(c * cw, (c + 1) * cw)
            s = jnp.dot(kvs[sub], qt_ref[0, 0, :, cols], preferred_element_type=F32)
            if sub == masked_sub:
                kpos = (kg * n_sub + sub) * tk + lax.broadcasted_iota(jnp.int32, (tk, cw), 0)
                qpos = qi * tq + ((c * cw + lax.broadcasted_iota(jnp.int32, (tk, cw), 1)) & (tq - 1))
                s = jnp.where(kpos <= qpos, s, -jnp.inf)
            s_buf[i % 3] = s
            m_prev = m_sc[:, cols]
            m_new = jnp.maximum(m_prev, jnp.max(s, axis=0, keepdims=True))
            m_sc[:, cols] = m_new
            return m_new, jnp.exp2(m_prev - m_new)

        def probs(i, m_new):
            p_buf[i % 2] = jnp.exp2((s_buf[i % 3] - m_new).astype(BF16))

        def values(i, alpha):
            sub, c = work[i]
            cols = slice(c * cw, (c + 1) * cw)
            acc_sc[:, cols] = alpha * acc_sc[:, cols] + jnp.dot(cts[sub], p_buf[i % 2],
                                                                preferred_element_type=F32)

        stats = {}
        for t in range(len(work) + 3):
            if t < len(work):
                stats[t] = scores(t)
            if 0 <= t - 2 < len(work):
                probs(t - 2, stats[t - 2][0])
            if 0 <= t - 3 < len(work):
                values(t - 3, stats[t - 3][1])

    first_kj = kg * n_sub

    @pl.when(first_kj + (n_sub - 1) < last_kj)
    def _():
        process(list(range(n_sub)))

    for diag in range(n_sub):
        @pl.when(first_kj + diag == last_kj)
        def _():
            process(list(range(diag + 1)), masked_sub=diag)

    @pl.when(kg == last_kj // n_sub)
    def _():
        for pair in range(MLA_HEADS // 2):
            halves = []
            for h in (2 * pair, 2 * pair + 1):
                cols = slice(h * tq, (h + 1) * tq)
                o_lat_t = acc_sc[0:KV_LORA, cols] * (1.0 / acc_sc[KV_LORA:KV_LORA + 1, cols])
                halves.append(jnp.dot(wuvt_ref[h], o_lat_t.astype(BF16), preferred_element_type=F32))
            o_ref[0, :, pair * LANES:(pair + 1) * LANES] = jnp.transpose(
                jnp.concatenate(halves, axis=0)).astype(o_ref.dtype)


def _mla_prompt_call(qt, kv, ct, w_uv_t, *, tq, tk):
    n_b, n_qb, _, n_cols = qt.shape
    seq = n_qb * tq
    assert tq & (tq - 1) == 0
    cw = min(n_cols, MLA_CHUNK_COLS)
    n_sub = MLA_TILES_PER_STEP if seq % (MLA_TILES_PER_STEP * tk) == 0 else 1
    qi_l, kj_l = [], []
    for i in range(n_qb):
        for g in range(((i * tq + tq - 1) // tk) // n_sub + 1):
            qi_l.append(i)
            kj_l.append(g)
    qi_t = jnp.asarray(np.array(qi_l, np.int32))
    kj_t = jnp.asarray(np.array(kj_l, np.int32))
    grid_spec = pltpu.PrefetchScalarGridSpec(
        num_scalar_prefetch=2,
        grid=(n_b, len(qi_l)),
        in_specs=[
            pl.BlockSpec((1, 1, QK_LAT, n_cols), lambda b, s, qi, kj: (b, qi[s], 0, 0)),
            pl.BlockSpec((1, n_sub * tk, QK_LAT), lambda b, s, qi, kj: (b, kj[s], 0)),
            pl.BlockSpec((1, KV_LORA, n_sub * tk), lambda b, s, qi, kj: (b, 0, kj[s])),
            pl.BlockSpec(w_uv_t.shape, lambda b, s, qi, kj: (0, 0, 0)),
        ],
        out_specs=pl.BlockSpec((1, tq, MLA_WIDTH), lambda b, s, qi, kj: (b, qi[s], 0)),
        scratch_shapes=[pltpu.VMEM((1, n_cols), F32), pltpu.VMEM((KV_LORA + ONES_ROWS, n_cols), F32),
                        pltpu.VMEM((3, tk, cw), F32), pltpu.VMEM((2, tk, cw), BF16)],
    )
    return pl.pallas_call(
        functools.partial(_mla_prompt_kernel, tq=tq, tk=tk, cw=cw, n_sub=n_sub),
        grid_spec=grid_spec,
        out_shape=jax.ShapeDtypeStruct((n_b, seq, MLA_WIDTH), BF16),
        compiler_params=pltpu.CompilerParams(dimension_semantics=("arbitrary", "arbitrary"),
                                             vmem_limit_bytes=VMEM_LIMIT),
        name="mla_prompt",
    )(qi_t, kj_t, qt, kv, ct, w_uv_t)


def _mla_decode_kernel(pt_ref, q_ref, kvn_ref, lat_hbm, krt_hbm, o_ref,
                       lat_buf, krt_buf, sems, *, n_pages, page, n_new, kc):
    b = pl.program_id(0)
    slot = b % 2
    rows = MLA_HEADS * n_new

    def page_copies(seq, slot_, i):
        pg = pt_ref[seq * n_pages + i]
        off = i * page if isinstance(i, int) else pl.multiple_of(i * page, page)
        return (pltpu.make_async_copy(lat_hbm.at[pg], lat_buf.at[slot_, pl.ds(off, page), :], sems.at[0, slot_]),
                pltpu.make_async_copy(krt_hbm.at[pg], krt_buf.at[slot_, i], sems.at[1, slot_]))

    def for_pages(seq, slot_, lo, hi, action):
        def body(i, carry):
            for cp in page_copies(seq, slot_, i):
                action(cp)
            return carry
        lax.fori_loop(lo, hi, body, 0, unroll=DMA_LOOP_UNROLL)

    start = lambda cp: cp.start()
    wait = lambda cp: cp.wait()
    last = pl.num_programs(0) - 1

    @pl.when(b == 0)
    def _():
        for_pages(0, 0, 0, n_pages, start)

    for_pages(b, slot, 0, n_pages, wait)
    nxt = jnp.minimum(b + 1, last)
    n_kc = n_pages * page // kc

    q = q_ref[0].reshape(rows, QK_LAT).astype(BF16)
    q_lat = q[:, :KV_LORA]
    q_rope = q[:, KV_LORA:]
    contract_last = (((1,), (1,)), ((), ()))
    lats, scores = [], []
    for c in range(n_kc):
        for i in range(c * n_pages // n_kc, (c + 1) * n_pages // n_kc):
            for cp in page_copies(nxt, 1 - slot, i):
                cp.start()
        lat = lat_buf[slot, c * kc:(c + 1) * kc, :].astype(BF16)
        krt = jnp.concatenate([krt_buf[slot, i] for i in range(c * kc // page, (c + 1) * kc // page)],
                              axis=1).astype(BF16)
        lats.append(lat)
        scores.append(lax.dot_general(q_lat, lat, contract_last, preferred_element_type=F32)
                      + jnp.dot(q_rope, krt, preferred_element_type=F32))
    kn = jnp.concatenate([kvn_ref[0], jnp.zeros((LANES - n_new, QK_LAT), F32)], axis=0).astype(BF16)
    sn = lax.dot_general(q, kn, contract_last, preferred_element_type=F32)
    tpos = lax.broadcasted_iota(jnp.int32, (rows, LANES), 0) & (n_new - 1)
    jpos = lax.broadcasted_iota(jnp.int32, (rows, LANES), 1)
    lats.append(kn[:, :KV_LORA])
    scores.append(jnp.where(jpos <= tpos, sn, -jnp.inf))

    m = functools.reduce(jnp.maximum, [jnp.max(s, axis=1, keepdims=True) for s in scores])
    probs = [jnp.exp2(s - m) for s in scores]
    l = functools.reduce(jnp.add, [jnp.sum(p, axis=1, keepdims=True) for p in probs])
    acc = functools.reduce(jnp.add, [jnp.dot(p.astype(BF16), v, preferred_element_type=F32)
                                     for p, v in zip(probs, lats)])
    o = acc * (1.0 / l)
    for h in range(MLA_HEADS):
        o_ref[0, :, h * KV_LORA:(h + 1) * KV_LORA] = o[h * n_new:(h + 1) * n_new, :].astype(o_ref.dtype)

    @pl.when(b == last)
    def _():
        for_pages(nxt, 1 - slot, 0, n_pages, wait)


def _mla_decode_call(q, kv_new, lat_pool, krt_pool, page_table):
    n_b, _, n_new, _ = q.shape
    assert n_new & (n_new - 1) == 0 and n_new <= LANES
    n_pages = page_table.shape[1]
    page = lat_pool.shape[1]
    past = n_pages * page
    kc = _pick_tile(past, DECODE_CHUNK_KEYS)
    grid_spec = pltpu.PrefetchScalarGridSpec(
        num_scalar_prefetch=1,
        grid=(n_b,),
        in_specs=[pl.BlockSpec((1, MLA_HEADS, n_new, QK_LAT), lambda b, pt: (b, 0, 0, 0)),
                  pl.BlockSpec((1, n_new, QK_LAT), lambda b, pt: (b, 0, 0)),
                  pl.BlockSpec(memory_space=pl.ANY),
                  pl.BlockSpec(memory_space=pl.ANY)],
        out_specs=pl.BlockSpec((1, n_new, LAT_WIDTH), lambda b, pt: (b, 0, 0)),
        scratch_shapes=[pltpu.VMEM((2, past, KV_LORA), F32), pltpu.VMEM((2, n_pages, QK_ROPE, page), F32),
                        pltpu.SemaphoreType.DMA((2, 2))],
    )
    return pl.pallas_call(
        functools.partial(_mla_decode_kernel, n_pages=n_pages, page=page, n_new=n_new, kc=kc),
        grid_spec=grid_spec,
        out_shape=jax.ShapeDtypeStruct((n_b, n_new, LAT_WIDTH), BF16),
        compiler_params=pltpu.CompilerParams(dimension_semantics=("arbitrary",),
                                             vmem_limit_bytes=VMEM_LIMIT),
        name="mla_decode",
    )(page_table.reshape(-1), q, kv_new, lat_pool, krt_pool)


def _gla_levels(chunk):
    levels = []
    h = chunk // 2
    while h >= 1:
        levels.append(h)
        h //= 2
    return levels


def _gla_decay_matrix(chunk):
    t = np.arange(chunk)[:, None]
    u = np.arange(chunk)[None, :]
    blocks = [(u <= t), (u > t)]
    for h in _gla_levels(chunk):
        off = t % (2 * h)
        mid = t - off + h
        upper = (off >= h) & (u >= mid) & (u <= t)
        lower = (off < h) & (u > t) & (u < mid)
        blocks.append(upper | lower)
    return np.concatenate(blocks, axis=0).astype(np.float32)


def _gla_chunk(qk, v, la, g_mat, gn, s_in, s_out, o_ref, r0, *, chunk):
    levels = _gla_levels(chunk)
    la_hi = la.astype(BF16)
    la_lo = (la - la_hi.astype(F32)).astype(BF16)
    e_all = jnp.dot(g_mat, jnp.concatenate([la_hi, la_lo], axis=0),
                    preferred_element_type=F32)
    q = qk[:, :GLA_QK] * (GLA_DK ** -0.5)
    k = qk[:, GLA_QK:]
    e_cum = e_all[0:chunk]
    q_in = q * jnp.exp(e_cum)
    k_out = k * jnp.exp(e_all[chunk:2 * chunk])
    row = lax.broadcasted_iota(jnp.int32, (chunk, GLA_QK), 0)
    q_lv, k_lv = [q], [k]
    for li, h in enumerate(levels):
        w = jnp.exp(e_all[(2 + li) * chunk:(3 + li) * chunk])
        upper = (row & (2 * h - 1)) >= h
        q_lv.append(jnp.where(upper, q * w, 0.0))
        k_lv.append(jnp.where(upper, 0.0, k * w))
    contract_last = (((1,), (1,)), ((), ()))
    contract_first = (((0,), (0,)), ((), ()))
    head_of_lane = lax.broadcasted_iota(jnp.int32, (chunk, GLA_QK), 1) // GLA_DK
    stack_heads = lambda x: jnp.concatenate(
        [jnp.where(head_of_lane == hd, x, 0.0) for hd in range(GLA_HEADS)], axis=0).astype(BF16)
    trow = lax.broadcasted_iota(jnp.int32, (GLA_HEADS * chunk, chunk), 0) & (chunk - 1)
    tcol = lax.broadcasted_iota(jnp.int32, (GLA_HEADS * chunk, chunk), 1)
    masks = [trow == tcol] + [(trow // (2 * h)) == (tcol // (2 * h)) for h in levels]
    a_all = jnp.zeros((GLA_HEADS * chunk, chunk), F32)
    for ql, kl, mk in zip(q_lv, k_lv, masks):
        a_all = a_all + jnp.where(mk, lax.dot_general(stack_heads(ql), kl.astype(BF16), contract_last,
                                                      preferred_element_type=F32), 0.0)
    a_all = a_all.astype(BF16)
    lane = lax.broadcasted_iota(jnp.int32, (chunk, LANES), 1)
    decay_row = e_cum[chunk - 1:chunk, :]
    for p in range(GLA_HEADS // 2):
        ls = slice(p * LANES, (p + 1) * LANES)
        s_pair = s_in[ls, :]
        decay = jnp.exp(jnp.transpose(jnp.broadcast_to(decay_row[:, ls], (LANES, LANES))))
        k_out_pair = k_out[:, ls].astype(BF16)
        new_rows = []
        for half in range(2):
            hd = 2 * p + half
            in_head = (lane // GLA_DK) == half
            v_h = v[:, hd * GLA_DV:(hd + 1) * GLA_DV]
            qi = jnp.where(in_head, q_in[:, ls], 0.0).astype(BF16)
            lhs = jnp.concatenate([qi, a_all[hd * chunk:(hd + 1) * chunk]], axis=1)
            rhs = jnp.concatenate([s_pair.astype(BF16), v_h], axis=0)
            o = jnp.dot(lhs, rhs, preferred_element_type=F32)
            o_ref[pl.ds(r0, chunk), hd * GLA_DV:(hd + 1) * GLA_DV] = (_rms(o) * gn).astype(o_ref.dtype)
            upd = lax.dot_general(k_out_pair, v_h, contract_first, preferred_element_type=F32)
            hr = slice(half * GLA_DK, (half + 1) * GLA_DK)
            new_rows.append(decay[hr, :] * s_pair[hr, :] + upd[hr, :])
        s_out[ls, :] = jnp.concatenate(new_rows, axis=0)


def _gla_kernel(*refs, chunk, n_sub, independent):
    if independent:
        qk_ref, v_ref, la_ref, g_ref, gn_ref, s0_ref, o_ref, st_ref = refs
    else:
        qk_ref, v_ref, la_ref, g_ref, gn_ref, o_ref, st_ref, s_sc = refs
        j = pl.program_id(1)

        @pl.when(j == 0)
        def _():
            s_sc[...] = jnp.zeros(s_sc.shape, F32)

    g_mat = g_ref[...]
    gn = gn_ref[...]

    def body(c, carry):
        r0 = pl.multiple_of(c * chunk, chunk)
        s_in, s_out = (s0_ref.at[c], st_ref.at[c]) if independent else (s_sc, s_sc)
        _gla_chunk(qk_ref[pl.ds(r0, chunk), :], v_ref[pl.ds(r0, chunk), :], la_ref[pl.ds(r0, chunk), :],
                   g_mat, gn, s_in, s_out, o_ref, r0, chunk=chunk)
        return carry

    lax.fori_loop(0, n_sub, body, 0, unroll=min(n_sub, GLA_UNROLL))

    if not independent:
        @pl.when(j == pl.num_programs(1) - 1)
        def _():
            st_ref[0] = s_sc[...]


def _gla_call(qk, v, la, gn, s0, *, n_seq, seq_len, rows, chunk):
    independent = s0 is not None
    g_np = _gla_decay_matrix(chunk)
    g_mat = jnp.asarray(np.concatenate([g_np, g_np], axis=1), BF16)
    full = lambda a: pl.BlockSpec(a.shape, lambda b, j: (0,) * a.ndim)
    if independent:
        assert seq_len == chunk and (n_seq * chunk) % rows == 0
        grid = (n_seq * chunk // rows, 1)
        tok = lambda width: pl.BlockSpec((rows, width), lambda b, j: (b, 0))
        state = pl.BlockSpec((rows // chunk, GLA_QK, GLA_DV), lambda b, j: (b, 0, 0))
        args, extra_specs, scratch = [qk, v, la, g_mat, gn, s0], [state], []
    else:
        per = seq_len // rows
        grid = (n_seq, per)
        tok = lambda width: pl.BlockSpec((rows, width), lambda b, j: (b * per + j, 0))
        state = pl.BlockSpec((1, GLA_QK, GLA_DV), lambda b, j: (b, 0, 0))
        args, extra_specs, scratch = [qk, v, la, g_mat, gn], [], [pltpu.VMEM((GLA_QK, GLA_DV), F32)]
    return pl.pallas_call(
        functools.partial(_gla_kernel, chunk=chunk, n_sub=rows // chunk, independent=independent),
        grid=grid,
        in_specs=[tok(2 * GLA_QK), tok(GLA_WIDTH), tok(GLA_QK), full(g_mat), full(gn)] + extra_specs,
        out_specs=(tok(GLA_WIDTH), state),
        out_shape=(jax.ShapeDtypeStruct((n_seq * seq_len, GLA_WIDTH), BF16),
                   jax.ShapeDtypeStruct((n_seq, GLA_QK, GLA_DV), F32)),
        scratch_shapes=scratch,
        compiler_params=pltpu.CompilerParams(dimension_semantics=("arbitrary", "arbitrary"),
                                             vmem_limit_bytes=VMEM_LIMIT),
        name="gla",
    )(*args)


def _out_kernel(x_ref, omla_ref, gm_ref, ogla_ref, gg_ref, p_ref, wuv_ref, wout_ref, wpg_ref, wpp_ref,
                gfin_ref, y_ref, *, latent_in):
    if latent_in:
        o_mla = jnp.dot(omla_ref[...], wuv_ref[...], preferred_element_type=F32)
    else:
        o_mla = omla_ref[...].astype(F32)
    o_mla = o_mla * gm_ref[...].astype(F32)
    o_gla = ogla_ref[...].astype(F32) * gg_ref[...].astype(F32)
    mix = jnp.concatenate([o_mla, o_gla], axis=1).astype(BF16)
    h = x_ref[...] + jnp.dot(mix, wout_ref[...], preferred_element_type=F32)
    gate = jnp.dot(h.astype(BF16), wpg_ref[...], preferred_element_type=F32)
    gate = 1.0 / (1.0 + jnp.exp(-gate))
    emb = jnp.dot(p_ref[...].astype(BF16), wpp_ref[...], preferred_element_type=F32)
    h = h + gate * emb
    y_ref[...] = _rms(h) * gfin_ref[...]


def _out_call(x2d, o_mla, gm, o_gla, gg, p2d, w, *, ts):
    n_tok = x2d.shape[0]
    mla_w = o_mla.shape[1]
    tok = lambda width: pl.BlockSpec((ts, width), lambda i: (i, 0))
    full = lambda a: pl.BlockSpec(a.shape, lambda i: (0,) * a.ndim)
    weights = (w["w_uv_bd"], w["w_out"], w["w_pg"], w["w_pp"], w["g_final"])
    return pl.pallas_call(
        functools.partial(_out_kernel, latent_in=(mla_w == LAT_WIDTH)),
        grid=(n_tok // ts,),
        in_specs=[tok(D_MODEL), tok(mla_w), tok(MLA_WIDTH), tok(GLA_WIDTH), tok(GLA_WIDTH), tok(P_DIM)]
        + [full(a) for a in weights],
        out_specs=tok(D_MODEL),
        out_shape=jax.ShapeDtypeStruct((n_tok, D_MODEL), F32),
        compiler_params=pltpu.CompilerParams(dimension_semantics=("arbitrary",),
                                             vmem_limit_bytes=VMEM_LIMIT),
        name="out",
    )(x2d, o_mla, gm, o_gla, gg, p2d, *weights)


def _prep_weights(g_mix_norm, w_in, g_qnorm, w_qup, g_kvnorm, w_uk, w_uv, w_gla_a2, b_gla_a, g_gla_onorm,
                  w_out, w_ple_gate, w_ple_proj, g_final):
    o_kv = Q_LORA
    o_gm = o_kv + KV_LORA + QK_ROPE
    o_gq = o_gm + MLA_WIDTH
    o_gk = o_gq + GLA_QK
    o_gv = o_gk + GLA_QK
    o_a = o_gv + GLA_WIDTH
    o_gg = o_a + GLA_GATE_RANK
    pad = jnp.zeros((D_MODEL, LANES - QK_ROPE - GLA_GATE_RANK), F32)
    w_all = jnp.concatenate([
        w_in[:, 0:o_gm], w_in[:, o_a:o_gg], pad,
        w_in[:, o_gm:o_gq],
        w_in[:, o_gq:o_gv],
        w_in[:, o_gv:o_a],
        w_in[:, o_gg:],
    ], axis=1).astype(BF16)
    assert w_all.shape == (D_MODEL, Z_WIDTH)
    w_qup3 = w_qup.reshape(Q_LORA, MLA_HEADS, QK_NOPE + QK_ROPE)
    eye = jnp.eye(MLA_HEADS, dtype=F32)
    w_uk_bd = (eye[:, None, :, None] * jnp.transpose(w_uk, (1, 2, 0))[:, :, None, :]
               ).reshape(MLA_HEADS * QK_NOPE, LAT_WIDTH)
    w_q = _absorb_call(w_qup3[:, :, :QK_NOPE].reshape(Q_LORA, MLA_HEADS * QK_NOPE), w_uk_bd,
                       w_qup3[:, :, QK_NOPE:].reshape(Q_LORA, MLA_HEADS * QK_ROPE))
    w_uv_bd =(eye[:, None, :, None] * jnp.transpose(w_uv, (1, 0, 2))[:, :, None, :]
               ).reshape(LAT_WIDTH, MLA_WIDTH).astype(BF16)
    w_a2 = jnp.zeros((LANES, GLA_QK), F32).at[QK_ROPE:QK_ROPE + GLA_GATE_RANK].set(w_gla_a2).astype(BF16)
    row = lambda a: a.reshape(1, -1).astype(F32)
    return dict(g_mix=row(g_mix_norm), w_all=w_all, g_qn=row(g_qnorm), w_q=w_q,
                g_kvn=row(g_kvnorm), w_a2=w_a2, b_a=row(b_gla_a), g_gla=row(g_gla_onorm),
                w_uv_bd=w_uv_bd, w_uv_t=jnp.transpose(w_uv, (1, 2, 0)).astype(BF16), w_out=w_out.astype(BF16), w_pg=w_ple_gate.astype(BF16),
                w_pp=w_ple_proj.astype(BF16), g_final=row(g_final))


def _rope_tables(pos):
    inv = 1.0 / (ROPE_THETA ** (jnp.arange(0, QK_ROPE, 2, dtype=F32) / QK_ROPE))
    ang = pos[:, None] * inv[None, :]
    cos, sin = jnp.cos(ang), jnp.sin(ang)
    reps = LANES // QK_ROPE
    return (jnp.tile(jnp.concatenate([cos, cos], axis=1), (1, reps)),
            jnp.tile(jnp.concatenate([-sin, sin], axis=1), (1, reps)))


def _pick_tile(n, pref):
    t = min(n, pref)
    while n % t:
        t //= 2
    return t


def kernel(x_prompt, x_sample, p_prompt, p_sample, cache_ckv, cache_krope, state_gla, page_table, g_mix_norm, w_in, g_qnorm, w_qup, g_kvnorm, w_uk, w_uv, w_gla_a2, b_gla_a, g_gla_onorm, w_out, w_ple_gate, w_ple_proj, g_final):
    n_b, seq, _ = x_prompt.shape
    n_d, n_new, _ = x_sample.shape
    assert w_in.shape[0] == 1, "single-layer stack"
    page = cache_ckv.shape[2]
    past_len = page_table.shape[1] * page
    w = _prep_weights(g_mix_norm[0], w_in[0], g_qnorm[0], w_qup[0], g_kvnorm[0], w_uk[0], w_uv[0],
                      w_gla_a2[0], b_gla_a[0], g_gla_onorm[0], w_out[0], w_ple_gate[0], w_ple_proj[0], g_final)

    ts_p = _pick_tile(seq, 512)
    cos_p, sin_p = _rope_tables(jnp.arange(seq, dtype=F32))
    xp = x_prompt.reshape(n_b * seq, D_MODEL)
    tq = _pick_tile(seq, 512)
    tk = _pick_tile(seq, 512)
    (qt_p, kv_p, ckv_p, kr_p, gm_p, gqk_p, gv_p, la_p, gg_p, ct_p) = _proj_call(
        xp, cos_p, sin_p, w, n_seq=n_b, seq_len=seq, ts=ts_p, q_dtype=BF16, table_period=seq // ts_p, tq=tq)
    omla_p = _mla_prompt_call(qt_p, kv_p.reshape(n_b, seq, QK_LAT), ct_p, w["w_uv_t"], tq=tq, tk=tk)
    chunk_p = min(GLA_CHUNK, seq)
    ogla_p, st_p = _gla_call(gqk_p, gv_p, la_p, w["g_gla"], None, n_seq=n_b, seq_len=seq,
                             rows=_pick_tile(seq, 1024), chunk=chunk_p)
    y_p = _out_call(xp, omla_p.reshape(n_b * seq, MLA_WIDTH), gm_p, ogla_p, gg_p,
                    p_prompt[0].reshape(n_b * seq, P_DIM), w, ts=ts_p)

    n_tok_s = n_d * n_new
    ts_s = _pick_tile(n_tok_s, 512)
    cos_s, sin_s = _rope_tables(past_len + jnp.arange(n_new, dtype=F32))
    cos_s = jnp.tile(cos_s, (ts_s // n_new, 1))
    sin_s = jnp.tile(sin_s, (ts_s // n_new, 1))
    xs = x_sample.reshape(n_tok_s, D_MODEL)
    (q_s, kv_s, ckv_s, kr_s, gm_s, gqk_s, gv_s, la_s, gg_s) = _proj_call(
        xs, cos_s, sin_s, w, n_seq=n_d, seq_len=n_new, ts=ts_s, q_dtype=F32, table_period=1)
    olat_s = _mla_decode_call(q_s, kv_s.reshape(n_d, n_new, QK_LAT), cache_ckv[0],
                              jnp.swapaxes(cache_krope[0], 1, 2), page_table)
    chunk_s = max(n_new, GLA_SAMPLE_CHUNK)
    pad_tok = lambda a: jnp.pad(a.reshape(n_d, n_new, -1), ((0, 0), (0, chunk_s - n_new), (0, 0))
                                ).reshape(n_d * chunk_s, -1)
    ogla_s, st_s = _gla_call(pad_tok(gqk_s), pad_tok(gv_s), pad_tok(la_s), w["g_gla"],
                             state_gla[0].reshape(n_d, GLA_QK, GLA_DV),
                             n_seq=n_d, seq_len=chunk_s, rows=chunk_s * _pick_tile(n_d, GLA_SAMPLE_SEQS),
                             chunk=chunk_s)
    ogla_s = ogla_s.reshape(n_d, chunk_s, GLA_WIDTH)[:, :n_new].reshape(n_tok_s, GLA_WIDTH)
    y_s = _out_call(xs, olat_s.reshape(n_tok_s, LAT_WIDTH), gm_s, ogla_s, gg_s,
                    p_sample[0].reshape(n_tok_s, P_DIM), w, ts=ts_s)

    return (y_p.reshape(n_b, seq, D_MODEL),
            y_s.reshape(n_d, n_new, D_MODEL),
            ckv_p.reshape(1, n_b, seq, KV_LORA),
            jnp.swapaxes(kr_p, 1, 2).reshape(1, n_b, seq, QK_ROPE),
            st_p.reshape(1, n_b, GLA_HEADS, GLA_DK, GLA_DV),
            ckv_s.reshape(1, n_d, n_new, KV_LORA),
            kr_s.reshape(1, n_d, n_new, QK_ROPE),
            st_s.reshape(1, n_d, GLA_HEADS, GLA_DK, GLA_DV))
```
